```python
import math
import jax, jax.numpy as jnp
from jax import lax
import numpy as np

D_MODEL = 1024
BATCH = 4
SEQ = 4096
DEPTH = 1

CONV_DIM = 1024
CONV_KERNEL = 31
N_HEADS = 8
QK_NOPE_DIM = 128
QK_ROPE_DIM = 64
V_HEAD_DIM = 128
Q_LORA_RANK = 384
KV_LORA_RANK = 256
ROPE_THETA = 10000.0
Q_BLOCK = 128
D_FF = 4 * D_MODEL
N_BRANCHES = 2
EPS = 1e-6

IN_WIDTHS = (2 * CONV_DIM, Q_LORA_RANK, KV_LORA_RANK, QK_ROPE_DIM, N_BRANCHES * D_MODEL)
IN_DIM = sum(IN_WIDTHS)
IN_SPLIT = tuple(int(v) for v in np.cumsum(IN_WIDTHS)[:-1])

kernel_name = "hybrid_conformer_mla_gated_block"


def rms_norm(x, g):
    xf = x.astype(jnp.float32)
    y = xf * lax.rsqrt(jnp.mean(xf * xf, axis=-1, keepdims=True) + EPS)
    return (y * g.astype(jnp.float32)).astype(x.dtype)


def layer_norm(x, g, b):
    xf = x.astype(jnp.float32)
    mu = jnp.mean(xf, axis=-1, keepdims=True)
    var = jnp.mean(jnp.square(xf - mu), axis=-1, keepdims=True)
    y = (xf - mu) * lax.rsqrt(var + EPS)
    return (y * g.astype(jnp.float32) + b.astype(jnp.float32)).astype(x.dtype)


def rope_tables(positions):
    inv_freq = 1.0 / (ROPE_THETA ** (jnp.arange(0, QK_ROPE_DIM, 2, dtype=jnp.float32) / QK_ROPE_DIM))
    ang = positions.astype(jnp.float32)[..., None] * inv_freq
    return jnp.cos(ang), jnp.sin(ang)


def apply_rope(t, cos, sin):
    t1, t2 = jnp.split(t, 2, axis=-1)
    cos = cos.astype(t.dtype)
    sin = sin.astype(t.dtype)
    return jnp.concatenate([t1 * cos - t2 * sin, t2 * cos + t1 * sin], axis=-1)


def causal_depthwise_conv(u, w, b):
    return lax.conv_general_dilated(
        u, w[:, None, :].astype(u.dtype), window_strides=(1,),
        padding=[(CONV_KERNEL - 1, 0)],
        dimension_numbers=("NWC", "WIO", "NWC"),
        feature_group_count=CONV_DIM) + b


def mla_attention(q_nope, q_rope, k_nope, k_rope, v):
    b, s, h, _ = q_nope.shape
    nb = s // Q_BLOCK
    scale = 1.0 / math.sqrt(QK_NOPE_DIM + QK_ROPE_DIM)
    qn = jnp.moveaxis(q_nope.reshape(b, nb, Q_BLOCK, h, QK_NOPE_DIM), 1, 0)
    qr = jnp.moveaxis(q_rope.reshape(b, nb, Q_BLOCK, h, QK_ROPE_DIM), 1, 0)
    k_idx = jnp.arange(s)

    def one_block(args):
        qn_b, qr_b, blk = args
        sc = (jnp.einsum("bqhd,bkhd->bhqk", qn_b, k_nope)
              + jnp.einsum("bqhr,bkr->bhqk", qr_b, k_rope)).astype(jnp.float32) * scale
        q_idx = blk * Q_BLOCK + jnp.arange(Q_BLOCK)
        causal = q_idx[:, None] >= k_idx[None, :]
        sc = jnp.where(causal[None, None], sc, -1e30)
        p = jax.nn.softmax(sc, axis=-1).astype(v.dtype)
        return jnp.einsum("bhqk,bkhd->bqhd", p, v)

    out = lax.map(one_block, (qn, qr, jnp.arange(nb)))
    return jnp.moveaxis(out, 0, 1).reshape(b, s, h * V_HEAD_DIM)


def token_mixer(h, cos, sin, w_in, conv_w, conv_b, conv_norm_g, conv_norm_b, w_conv_out,
                q_norm_g, w_uq, kv_norm_g, w_ukv, w_attn_out, w_out):
    b, s, _ = h.shape
    z = h @ w_in
    u_glu, q_lat, kv_lat, k_r, gate_logits = jnp.split(z, IN_SPLIT, axis=-1)

    ua, ub = jnp.split(u_glu, 2, axis=-1)
    u = ua * jax.nn.sigmoid(ub)
    u = causal_depthwise_conv(u, conv_w, conv_b)
    u = jax.nn.silu(layer_norm(u, conv_norm_g, conv_norm_b))
    y_a = u @ w_conv_out

    q = (rms_norm(q_lat, q_norm_g) @ w_uq).reshape(b, s, N_HEADS, QK_NOPE_DIM + QK_ROPE_DIM)
    q_nope, q_rope = jnp.split(q, [QK_NOPE_DIM], axis=-1)
    q_rope = apply_rope(q_rope, cos[:, :, None, :], sin[:, :, None, :])
    kv = (rms_norm(kv_lat, kv_norm_g) @ w_ukv).reshape(b, s, N_HEADS, QK_NOPE_DIM + V_HEAD_DIM)
    k_nope, v = jnp.split(kv, [QK_NOPE_DIM], axis=-1)
    k_rope = apply_rope(k_r, cos, sin)
    o = mla_attention(q_nope, q_rope, k_nope, k_rope, v)
    y_b = o @ w_attn_out

    g_a, g_b = jnp.split(jax.nn.sigmoid(gate_logits), N_BRANCHES, axis=-1)
    return (g_a * y_a + g_b * y_b) @ w_out


def setup_inputs(seed: int = 0) -> dict:
    key = jax.random.key(seed)
    ks = jax.random.split(key, 32)
    L = DEPTH

    def nrm(k, shape, fan_in, mult=1.0):
        return jax.random.normal(k, shape, jnp.float32) * (mult * fan_in ** -0.5)

    def gain(k, shape):
        return 1.0 + 0.05 * jax.random.normal(k, shape, jnp.float32)

    offsets = jax.random.randint(ks[2], (BATCH, 1), 0, 1024, dtype=jnp.int32)
    positions = offsets + jnp.arange(SEQ, dtype=jnp.int32)[None, :]
    return {
        "x": jax.random.normal(ks[0], (BATCH, SEQ, D_MODEL), jnp.float32),
        "c": jax.random.normal(ks[1], (BATCH, D_MODEL), jnp.float32),
        "positions": positions,
        "w_ada": nrm(ks[3], (L, D_MODEL, 6 * D_MODEL), D_MODEL, 0.2),
        "b_ada": 0.01 * jax.random.normal(ks[4], (L, 6 * D_MODEL), jnp.float32),
        "g_pre_mix": gain(ks[5], (L, D_MODEL)),
        "g_post_mix": gain(ks[6], (L, D_MODEL)),
        "g_pre_mlp": gain(ks[7], (L, D_MODEL)),
        "g_post_mlp": gain(ks[8], (L, D_MODEL)),
        "w_in": nrm(ks[9], (L, D_MODEL, IN_DIM), D_MODEL),
        "conv_w": nrm(ks[10], (L, CONV_KERNEL, CONV_DIM), CONV_KERNEL),
        "conv_b": 0.01 * jax.random.normal(ks[11], (L, CONV_DIM), jnp.float32),
        "conv_norm_g": gain(ks[12], (L, CONV_DIM)),
        "conv_norm_b": 0.01 * jax.random.normal(ks[13], (L, CONV_DIM), jnp.float32),
        "w_conv_out": nrm(ks[14], (L, CONV_DIM, D_MODEL), CONV_DIM),
        "q_norm_g": gain(ks[15], (L, Q_LORA_RANK)),
        "w_uq": nrm(ks[16], (L, Q_LORA_RANK, N_HEADS * (QK_NOPE_DIM + QK_ROPE_DIM)), Q_LORA_RANK),
        "kv_norm_g": gain(ks[17], (L, KV_LORA_RANK)),
        "w_ukv": nrm(ks[18], (L, KV_LORA_RANK, N_HEADS * (QK_NOPE_DIM + V_HEAD_DIM)), KV_LORA_RANK),
        "w_attn_out": nrm(ks[19], (L, N_HEADS * V_HEAD_DIM, D_MODEL), N_HEADS * V_HEAD_DIM),
        "w_out": nrm(ks[20], (L, D_MODEL, D_MODEL), D_MODEL),
        "w_mlp_in": nrm(ks[21], (L, D_MODEL, D_FF), D_MODEL),
        "w_mlp_out": nrm(ks[22], (L, D_FF, D_MODEL), D_FF),
    }


def reference(x, c, positions, w_ada, b_ada, g_pre_mix, g_post_mix, g_pre_mlp, g_post_mlp,
              w_in, conv_w, conv_b, conv_norm_g, conv_norm_b, w_conv_out,
              q_norm_g, w_uq, kv_norm_g, w_ukv, w_attn_out, w_out, w_mlp_in, w_mlp_out):
    cos, sin = rope_tables(positions)
    c_act = jax.nn.silu(c)
    for l in range(DEPTH):
        mod = c_act @ w_ada[l] + b_ada[l]
        shift1, scale1, gate1, shift2, scale2, gate2 = [m[:, None, :] for m in jnp.split(mod, 6, axis=-1)]

        h = rms_norm(x, g_pre_mix[l]) * (1.0 + scale1) + shift1
        y = token_mixer(h, cos, sin, w_in[l], conv_w[l], conv_b[l], conv_norm_g[l], conv_norm_b[l],
                        w_conv_out[l], q_norm_g[l], w_uq[l], kv_norm_g[l], w_ukv[l],
                        w_attn_out[l], w_out[l])
        x = x + gate1 * rms_norm(y, g_post_mix[l])

        h = rms_norm(x, g_pre_mlp[l]) * (1.0 + scale2) + shift2
        y = jnp.square(jax.nn.relu(h @ w_mlp_in[l])) @ w_mlp_out[l]
        x = x + gate2 * rms_norm(y, g_post_mlp[l])
    return x
```

```python
import functools
import math

import jax
import jax.numpy as jnp
import numpy as np
from jax import lax
from jax.experimental import pallas as pl
from jax.experimental.pallas import tpu as pltpu

D_MODEL = 1024
CONV_DIM = 1024
CONV_KERNEL = 31
N_HEADS = 8
QK_NOPE_DIM = 128
QK_ROPE_DIM = 64
V_HEAD_DIM = 128
Q_LORA_RANK = 384
KV_LORA_RANK = 256
ROPE_THETA = 10000.0
D_FF = 4 * D_MODEL
EPS = 1e-6

LANES = 128
QK_PAD_DIM = 2 * LANES
CONV_HALO = 32
VMEM_LIMIT = 56 * 1024 * 1024

_C_GLU = 0
_C_GATE = 2 * CONV_DIM
_C_QLAT = _C_GATE + 2 * D_MODEL
_C_KVLAT = _C_QLAT + Q_LORA_RANK
_C_KR = _C_KVLAT + KV_LORA_RANK
_C_END = _C_KR + LANES

BF16 = jnp.bfloat16
F32 = jnp.float32


def _dot(a, b):
    return jnp.dot(a, b, preferred_element_type=F32)


def _rms(v):
    return v * lax.rsqrt(jnp.mean(v * v, axis=-1, keepdims=True) + EPS)


def _params(*sem):
    return pltpu.CompilerParams(dimension_semantics=sem, vmem_limit_bytes=VMEM_LIMIT)


def _const_spec(shape):
    nd = len(shape)
    return pl.BlockSpec(shape, lambda *_: (0,) * nd, pipeline_mode=pl.Buffered(1))


def _ada_kernel(c_ref, w_ref, b_ref, o_ref):
    c = c_ref[...]
    act = c * jax.nn.sigmoid(c)
    o_ref[...] = _dot(act.astype(BF16), w_ref[...].astype(BF16)) + b_ref[...]


def _ada(c_pad, w_ada, b_ada):
    rows, d = c_pad.shape
    n = w_ada.shape[1]
    tn = 1024
    return pl.pallas_call(
        _ada_kernel,
        grid=(n // tn,),
        in_specs=[
            pl.BlockSpec((rows, d), lambda j: (0, 0)),
            pl.BlockSpec((d, tn), lambda j: (0, j)),
            pl.BlockSpec((1, tn), lambda j: (0, j)),
        ],
        out_specs=pl.BlockSpec((rows, tn), lambda j: (0, j)),
        out_shape=jax.ShapeDtypeStruct((rows, n), F32),
        compiler_params=_params("arbitrary"),
        name="ada",
    )(c_pad, w_ada, b_ada)


def _rope_pairs(t, cos, sin, first_half):
    rot = jnp.where(first_half, -pltpu.roll(t, LANES - QK_ROPE_DIM // 2, 1),
                    pltpu.roll(t, QK_ROPE_DIM // 2, 1))
    return t * cos + rot * sin


def _in_proj_kernel(x_ref, mod_ref, gpre_ref, win_ref, qg_ref, wuq_ref, kvg_ref, wukv_ref,
                    pos_ref, invf_ref, u_ref, gate_ref, q_ref, k_ref, v_ref, *, q_scale):
    x = x_ref[...]
    shift = mod_ref[0, 0:1, :]
    scale = mod_ref[0, 1:2, :]
    h = (_rms(x) * gpre_ref[...]) * (1.0 + scale) + shift
    hb = h.astype(BF16)

    zg = _dot(hb, win_ref[:, _C_GLU:_C_GATE])
    u_ref[...] = (zg[:, :CONV_DIM] * jax.nn.sigmoid(zg[:, CONV_DIM:])).astype(BF16)

    gate_ref[...] = jax.nn.sigmoid(_dot(hb, win_ref[:, _C_GATE:_C_QLAT])).astype(BF16)

    ang = pos_ref[...] * invf_ref[...]
    cos = jnp.cos(ang)
    sin = jnp.sin(ang)
    lane = lax.broadcasted_iota(jnp.int32, ang.shape, 1)
    first_half = (lane % QK_ROPE_DIM) < (QK_ROPE_DIM // 2)
    low = lane < QK_ROPE_DIM

    ql = _dot(hb, win_ref[:, _C_QLAT:_C_KVLAT])
    qn = (_rms(ql) * qg_ref[...]).astype(BF16)
    q = _dot(qn, wuq_ref[...]) * q_scale
    nope_w = N_HEADS * QK_NOPE_DIM
    for hp in range(N_HEADS // 2):
        qr = _rope_pairs(q[:, nope_w + LANES * hp: nope_w + LANES * (hp + 1)], cos, sin, first_half)
        qr_hi = pltpu.roll(qr, QK_ROPE_DIM, 1)
        for j, part in ((0, qr), (1, qr_hi)):
            hd = 2 * hp + j
            q_ref[0, hd, :, 0:LANES] = q[:, LANES * hd: LANES * (hd + 1)].astype(BF16)
            q_ref[0, hd, :, LANES:QK_PAD_DIM] = jnp.where(low, part, 0.0).astype(BF16)

    kvl = _dot(hb, win_ref[:, _C_KVLAT:_C_KR])
    kvn = (_rms(kvl) * kvg_ref[...]).astype(BF16)
    kv = _dot(kvn, wukv_ref[...])
    kr = _rope_pairs(_dot(hb, win_ref[:, _C_KR:_C_END]), cos, sin, first_half)
    kr = jnp.where(low, kr, 0.0).astype(BF16)
    for hd in range(N_HEADS):
        k_ref[0, hd, :, 0:LANES] = kv[:, LANES * hd: LANES * (hd + 1)].astype(BF16)
        k_ref[0, hd, :, LANES:QK_PAD_DIM] = kr
        v_ref[0, hd, :, :] = kv[:, nope_w + LANES * hd: nope_w + LANES * (hd + 1)].astype(BF16)


def _in_proj(x2, mod3, g_pre, w_in_r, q_norm_g, w_uq_r, kv_norm_g, w_ukv_r, pos128, invf128,
             batch, seq, tm):
    nt = seq // tm
    q_scale = (1.0 / math.sqrt(QK_NOPE_DIM + QK_ROPE_DIM)) * math.log2(math.e)
    tok = lambda i: (i, 0)
    heads = lambda i: (i // nt, 0, i % nt, 0)
    return pl.pallas_call(
        functools.partial(_in_proj_kernel, q_scale=q_scale),
        grid=(batch * nt,),
        in_specs=[
            pl.BlockSpec((tm, D_MODEL), tok),
            pl.BlockSpec((1, 6, D_MODEL), lambda i: (i // nt, 0, 0)),
            _const_spec((1, D_MODEL)),
            _const_spec(w_in_r.shape),
            _const_spec((1, Q_LORA_RANK)),
            _const_spec(w_uq_r.shape),
            _const_spec((1, KV_LORA_RANK)),
            _const_spec(w_ukv_r.shape),
            pl.BlockSpec((tm, LANES), tok),
            _const_spec((1, LANES)),
        ],
        out_specs=[
            pl.BlockSpec((tm, CONV_DIM), tok),
            pl.BlockSpec((tm, 2 * D_MODEL), tok),
            pl.BlockSpec((1, N_HEADS, tm, QK_PAD_DIM), heads),
            pl.BlockSpec((1, N_HEADS, tm, QK_PAD_DIM), heads),
            pl.BlockSpec((1, N_HEADS, tm, V_HEAD_DIM), heads),
        ],
        out_shape=[
            jax.ShapeDtypeStruct((batch * seq, CONV_DIM), BF16),
            jax.ShapeDtypeStruct((batch * seq, 2 * D_MODEL), BF16),
            jax.ShapeDtypeStruct((batch, N_HEADS, seq, QK_PAD_DIM), BF16),
            jax.ShapeDtypeStruct((batch, N_HEADS, seq, QK_PAD_DIM), BF16),
            jax.ShapeDtypeStruct((batch, N_HEADS, seq, V_HEAD_DIM), BF16),
        ],
        compiler_params=_params("arbitrary"),
        name="in_proj",
    )(x2, mod3, g_pre, w_in_r, q_norm_g, w_uq_r, kv_norm_g, w_ukv_r, pos128, invf128)


def _conv_kernel(u_ref, halo_ref, w_ref, b_ref, lng_ref, lnb_ref, wout_ref, o_ref, buf_ref, act_ref, *,
                 ts, rows):
    t = pl.program_id(1)
    halo = halo_ref[0].astype(F32)
    buf_ref[0:CONV_HALO, :] = jnp.where(t > 0, halo, 0.0)
    buf_ref[CONV_HALO:, :] = u_ref[0].astype(F32)
    first = CONV_HALO - (CONV_KERNEL - 1)
    for r0 in range(0, ts, rows):
        acc = jnp.broadcast_to(b_ref[...], (rows, CONV_DIM))
        for k in range(CONV_KERNEL):
            acc = acc + w_ref[k:k + 1, :] * buf_ref[r0 + first + k: r0 + first + k + rows, :]
        mu = jnp.mean(acc, axis=-1, keepdims=True)
        cen = acc - mu
        var = jnp.mean(cen * cen, axis=-1, keepdims=True)
        y = cen * lax.rsqrt(var + EPS) * lng_ref[...] + lnb_ref[...]
        y = y * jax.nn.sigmoid(y)
        act_ref[r0:r0 + rows, :] = y.astype(BF16)
    o_ref[0] = _dot(act_ref[...], wout_ref[...]).astype(BF16)


def _conv(u3, conv_w, conv_b, ln_g, ln_b, w_conv_out, ts, rows):
    batch, seq, _ = u3.shape
    hb = ts // CONV_HALO
    return pl.pallas_call(
        functools.partial(_conv_kernel, ts=ts, rows=rows),
        grid=(batch, seq // ts),
        in_specs=[
            pl.BlockSpec((1, ts, CONV_DIM), lambda b, t: (b, t, 0)),
            pl.BlockSpec((1, CONV_HALO, CONV_DIM), lambda b, t: (b, jnp.maximum(t * hb - 1, 0), 0)),
            _const_spec(conv_w.shape),
            _const_spec((1, CONV_DIM)),
            _const_spec((1, CONV_DIM)),
            _const_spec((1, CONV_DIM)),
            _const_spec(w_conv_out.shape),
        ],
        out_specs=pl.BlockSpec((1, ts, D_MODEL), lambda b, t: (b, t, 0)),
        out_shape=jax.ShapeDtypeStruct((batch, seq, D_MODEL), BF16),
        scratch_shapes=[pltpu.VMEM((CONV_HALO + ts, CONV_DIM), F32), pltpu.VMEM((ts, CONV_DIM), BF16)],
        compiler_params=_params("arbitrary", "arbitrary"),
        name="conv",
    )(u3, u3, conv_w, conv_b, ln_g, ln_b, w_conv_out)


def _attn_kernel(q_ref, k_ref, v_ref, o_ref, *, tq, tk):
    qi = pl.program_id(2)
    q = q_ref[0, 0]

    def step(j, carry, masked):
        m, l, acc = carry
        start = pl.multiple_of(j * tk, tk)
        k = k_ref[0, 0, pl.ds(start, tk), :]
        v = v_ref[0, 0, pl.ds(start, tk), :]
        s = lax.dot_general(q, k, (((1,), (1,)), ((), ())), preferred_element_type=F32)
        if masked:
            rows = lax.broadcasted_iota(jnp.int32, (tq, tk), 0) + qi * tq
            cols = lax.broadcasted_iota(jnp.int32, (tq, tk), 1) + start
            s = jnp.where(rows >= cols, s, -1e30)
        m_new = jnp.maximum(m, jnp.max(s, axis=-1, keepdims=True))
        alpha = jnp.exp2(m - m_new)
        p = jnp.exp2(s - m_new)
        l = alpha * l + jnp.sum(p, axis=-1, keepdims=True)
        acc = alpha * acc + _dot(p.astype(BF16), v)
        return m_new, l, acc

    init = (jnp.full((tq, 1), -1e30, F32), jnp.zeros((tq, 1), F32),
            jnp.zeros((tq, V_HEAD_DIM), F32))
    n_full = qi * (tq // tk)
    carry = lax.fori_loop(0, n_full, lambda j, c: step(j, c, False), init)
    for d in range(tq // tk):
        carry = step(n_full + d, carry, True)
    _, l, acc = carry
    o_ref[0] = (acc / l).astype(BF16)


def _attn(q4, k4, v4, tq, tk):
    batch, heads, seq, _ = q4.shape
    return pl.pallas_call(
        functools.partial(_attn_kernel, tq=tq, tk=tk),
        grid=(batch, heads, seq // tq),
        in_specs=[
            pl.BlockSpec((1, 1, tq, QK_PAD_DIM), lambda b, h, i: (b, h, i, 0)),
            pl.BlockSpec((1, 1, seq, QK_PAD_DIM), lambda b, h, i: (b, h, 0, 0)),
            pl.BlockSpec((1, 1, seq, V_HEAD_DIM), lambda b, h, i: (b, h, 0, 0)),
        ],
        out_specs=pl.BlockSpec((1, tq, V_HEAD_DIM), lambda b, h, i: (b, i, h)),
        out_shape=jax.ShapeDtypeStruct((batch, seq, heads * V_HEAD_DIM), BF16),
        compiler_params=_params("arbitrary", "arbitrary", "arbitrary"),
        name="attn",
    )(q4, k4, v4)


def _mix_out_kernel(o_ref, ya_ref, gate_ref, x_ref, mod_ref, gpost_ref, wao_ref, wout_ref, x1_ref):
    y_b = _dot(o_ref[...], wao_ref[...])
    g_a = gate_ref[:, :D_MODEL].astype(F32)
    g_b = gate_ref[:, D_MODEL:].astype(F32)
    merged = g_a * ya_ref[...].astype(F32) + g_b * y_b
    y = _dot(merged.astype(BF16), wout_ref[...])
    gate1 = mod_ref[0, 2:3, :]
    x1_ref[...] = x_ref[...] + gate1 * (_rms(y) * gpost_ref[...])


def _mix_out(o2, ya2, gates, x2, mod3, g_post, w_attn_out, w_out, seq, tm):
    tokens = x2.shape[0]
    nt = seq // tm
    tok = lambda i: (i, 0)
    return pl.pallas_call(
        _mix_out_kernel,
        grid=(tokens // tm,),
        in_specs=[
            pl.BlockSpec((tm, D_MODEL), tok),
            pl.BlockSpec((tm, D_MODEL), tok),
            pl.BlockSpec((tm, 2 * D_MODEL), tok),
            pl.BlockSpec((tm, D_MODEL), tok),
            pl.BlockSpec((1, 6, D_MODEL), lambda i: (i // nt, 0, 0)),
            _const_spec((1, D_MODEL)),
            _const_spec(w_attn_out.shape),
            _const_spec(w_out.shape),
        ],
        out_specs=pl.BlockSpec((tm, D_MODEL), tok),
        out_shape=jax.ShapeDtypeStruct((tokens, D_MODEL), F32),
        compiler_params=_params("arbitrary"),
        name="mix_out",
    )(o2, ya2, gates, x2, mod3, g_post, w_attn_out, w_out)


def _mlp_kernel(x_ref, mod_ref, gpre_ref, gpost_ref, w1_ref, w2_ref, o_ref):
    x = x_ref[...]
    shift = mod_ref[0, 3:4, :]
    scale = mod_ref[0, 4:5, :]
    gate = mod_ref[0, 5:6, :]
    h = (_rms(x) * gpre_ref[...]) * (1.0 + scale) + shift
    hid = jnp.maximum(_dot(h.astype(BF16), w1_ref[...]), 0.0)
    y = _dot((hid * hid).astype(BF16), w2_ref[...])
    o_ref[...] = x + gate * (_rms(y) * gpost_ref[...])


def _mlp(x1, mod3, g_pre, g_post, w1, w2, seq, tm):
    tokens = x1.shape[0]
    nt = seq // tm
    tok = lambda i: (i, 0)
    return pl.pallas_call(
        _mlp_kernel,
        grid=(tokens // tm,),
        in_specs=[
            pl.BlockSpec((tm, D_MODEL), tok),
            pl.BlockSpec((1, 6, D_MODEL), lambda i: (i // nt, 0, 0)),
            _const_spec((1, D_MODEL)),
            _const_spec((1, D_MODEL)),
            _const_spec(w1.shape),
            _const_spec(w2.shape),
        ],
        out_specs=pl.BlockSpec((tm, D_MODEL), tok),
        out_shape=jax.ShapeDtypeStruct((tokens, D_MODEL), F32),
        compiler_params=_params("arbitrary"),
        name="mlp",
    )(x1, mod3, g_pre, g_post, w1, w2)


def _head_major_cols(w, split):
    k = w.shape[0]
    w3 = w.reshape(k, N_HEADS, -1)
    return jnp.concatenate([w3[:, :, :split].reshape(k, -1), w3[:, :, split:].reshape(k, -1)], axis=1)


def kernel(x, c, positions, w_ada, b_ada, g_pre_mix, g_post_mix, g_pre_mlp, g_post_mlp, w_in, conv_w, conv_b, conv_norm_g, conv_norm_b, w_conv_out, q_norm_g, w_uq, kv_norm_g, w_ukv, w_attn_out, w_out, w_mlp_in, w_mlp_out):
    batch, seq, d = x.shape
    depth = w_ada.shape[0]
    tokens = batch * seq
    row = lambda v: v.reshape(1, -1)

    inv_freq = 1.0 / (ROPE_THETA ** (jnp.arange(0, QK_ROPE_DIM, 2, dtype=F32) / QK_ROPE_DIM))
    invf128 = jnp.tile(inv_freq, LANES // (QK_ROPE_DIM // 2)).reshape(1, LANES)
    pos128 = jnp.broadcast_to(positions.astype(F32).reshape(tokens, 1), (tokens, LANES))
    c_pad = jnp.pad(c, ((0, 8 - batch), (0, 0)))

    x2 = x.reshape(tokens, d)
    for l in range(depth):
        wi = w_in[l]
        s0, s1, s2, s3 = 2 * CONV_DIM, 2 * CONV_DIM + Q_LORA_RANK, 2 * CONV_DIM + Q_LORA_RANK + KV_LORA_RANK, \
            2 * CONV_DIM + Q_LORA_RANK + KV_LORA_RANK + QK_ROPE_DIM
        w_in_r = jnp.concatenate([wi[:, :s0], wi[:, s3:], wi[:, s0:s1], wi[:, s1:s2], wi[:, s2:s3], wi[:, s2:s3]],
                                 axis=1).astype(BF16)
        w_uq_r = _head_major_cols(w_uq[l], QK_NOPE_DIM).astype(BF16)
        w_ukv_r = _head_major_cols(w_ukv[l], QK_NOPE_DIM).astype(BF16)

        mod = _ada(c_pad, w_ada[l], row(b_ada[l]))
        mod3 = mod[:batch].reshape(batch, 6, d)

        u, gates, q4, k4, v4 = _in_proj(x2, mod3, row(g_pre_mix[l]), w_in_r, row(q_norm_g[l]), w_uq_r,
                                        row(kv_norm_g[l]), w_ukv_r, pos128, invf128, batch, seq, tm=256)
        y_a = _conv(u.reshape(batch, seq, CONV_DIM), conv_w[l], row(conv_b[l]), row(conv_norm_g[l]),
                    row(conv_norm_b[l]), w_conv_out[l].astype(BF16), ts=512, rows=32)
        o = _attn(q4, k4, v4, tq=512, tk=512)
        x2 = _mix_out(o.reshape(tokens, d), y_a.reshape(tokens, d), gates, x2, mod3, row(g_post_mix[l]),
                      w_attn_out[l].astype(BF16), w_out[l].astype(BF16), seq, tm=512)
        x2 = _mlp(x2, mod3, row(g_pre_mlp[l]), row(g_post_mlp[l]), w_mlp_in[l].astype(BF16),
                  w_mlp_out[l].astype(BF16), seq, tm=512)
    return x2.reshape(batch, seq, d)
```

```python
import functools
import math

import jax
import jax.numpy as jnp
import numpy as np
from jax import lax
from jax.experimental import pallas as pl
from jax.experimental.pallas import tpu as pltpu

D_MODEL = 1024
CONV_DIM = 1024
CONV_KERNEL = 31
N_HEADS = 8
QK_NOPE_DIM = 128
QK_ROPE_DIM = 64
V_HEAD_DIM = 128
Q_LORA_RANK = 384
KV_LORA_RANK = 256
ROPE_THETA = 10000.0
D_FF = 4 * D_MODEL
EPS = 1e-6

LANES = 128
QK_PAD_DIM = 2 * LANES
CONV_HALO = 32
VMEM_LIMIT = 56 * 1024 * 1024

_C_GLU = 0
_C_GATE = 2 * CONV_DIM
_C_QLAT = _C_GATE + 2 * D_MODEL
_C_KVLAT = _C_QLAT + Q_LORA_RANK
_C_KR = _C_KVLAT + KV_LORA_RANK
_C_END = _C_KR + LANES

BF16 = jnp.bfloat16
F32 = jnp.float32


def _dot(a, b):
    return jnp.dot(a, b, preferred_element_type=F32)


def _rms(v):
    return v * lax.rsqrt(jnp.mean(v * v, axis=-1, keepdims=True) + EPS)


def _params(*sem):
    return pltpu.CompilerParams(dimension_semantics=sem, vmem_limit_bytes=VMEM_LIMIT)


def _const_spec(shape):
    nd = len(shape)
    return pl.BlockSpec(shape, lambda *_: (0,) * nd, pipeline_mode=pl.Buffered(1))


def _ada_kernel(c_ref, w_ref, b_ref, o_ref):
    c = c_ref[...]
    act = c * jax.nn.sigmoid(c)
    o_ref[...] = _dot(act.astype(BF16), w_ref[...].astype(BF16)) + b_ref[...]


def _ada(c_pad, w_ada, b_ada):
    rows, d = c_pad.shape
    n = w_ada.shape[1]
    tn = 1024
    return pl.pallas_call(
        _ada_kernel,
        grid=(n // tn,),
        in_specs=[
            pl.BlockSpec((rows, d), lambda j: (0, 0)),
            pl.BlockSpec((d, tn), lambda j: (0, j)),
            pl.BlockSpec((1, tn), lambda j: (0, j)),
        ],
        out_specs=pl.BlockSpec((rows, tn), lambda j: (0, j)),
        out_shape=jax.ShapeDtypeStruct((rows, n), F32),
        compiler_params=_params("arbitrary"),
        name="ada",
    )(c_pad, w_ada, b_ada)


def _rope_pairs(t, cos, sin, first_half):
    rot = jnp.where(first_half, -pltpu.roll(t, LANES - QK_ROPE_DIM // 2, 1),
                    pltpu.roll(t, QK_ROPE_DIM // 2, 1))
    return t * cos + rot * sin


def _in_proj_kernel(x_ref, mod_ref, gpre_ref, win_ref, qg_ref, wuq_ref, kvg_ref, wuk_ref, wuvt_ref,
                    pos_ref, invf_ref, u_ref, gate_ref, q_ref, k_ref, vt_ref, *, q_scale):
    x = x_ref[...]
    shift = mod_ref[0, 0:1, :]
    scale = mod_ref[0, 1:2, :]
    h = (_rms(x) * gpre_ref[...]) * (1.0 + scale) + shift
    hb = h.astype(BF16)

    zg = _dot(hb, win_ref[:, _C_GLU:_C_GATE])
    u_ref[...] = (zg[:, :CONV_DIM] * jax.nn.sigmoid(zg[:, CONV_DIM:])).astype(BF16)

    gate_ref[...] = jax.nn.sigmoid(_dot(hb, win_ref[:, _C_GATE:_C_QLAT])).astype(BF16)

    ang = pos_ref[...] * invf_ref[...]
    cos = jnp.cos(ang)
    sin = jnp.sin(ang)
    lane = lax.broadcasted_iota(jnp.int32, ang.shape, 1)
    first_half = (lane % QK_ROPE_DIM) < (QK_ROPE_DIM // 2)
    low = lane < QK_ROPE_DIM

    ql = _dot(hb, win_ref[:, _C_QLAT:_C_KVLAT])
    qn = (_rms(ql) * qg_ref[...]).astype(BF16)
    q = _dot(qn, wuq_ref[...]) * q_scale
    nope_w = N_HEADS * QK_NOPE_DIM
    for hp in range(N_HEADS // 2):
        qr = _rope_pairs(q[:, nope_w + LANES * hp: nope_w + LANES * (hp + 1)], cos, sin, first_half)
        qr_hi = pltpu.roll(qr, QK_ROPE_DIM, 1)
        for j, part in ((0, qr), (1, qr_hi)):
            hd = 2 * hp + j
            q_ref[0, hd, :, 0:LANES] = q[:, LANES * hd: LANES * (hd + 1)].astype(BF16)
            q_ref[0, hd, :, LANES:QK_PAD_DIM] = jnp.where(low, part, 0.0).astype(BF16)

    kvl = _dot(hb, win_ref[:, _C_KVLAT:_C_KR])
    kvn = (_rms(kvl) * kvg_ref[...]).astype(BF16)
    kn = _dot(kvn, wuk_ref[...])
    kr = _rope_pairs(_dot(hb, win_ref[:, _C_KR:_C_END]), cos, sin, first_half)
    kr = jnp.where(low, kr, 0.0).astype(BF16)
    for hd in range(N_HEADS):
        k_ref[0, hd, :, 0:LANES] = kn[:, LANES * hd: LANES * (hd + 1)].astype(BF16)
        k_ref[0, hd, :, LANES:QK_PAD_DIM] = kr
    vt = lax.dot_general(wuvt_ref[...], kvn, (((1,), (1,)), ((), ())), preferred_element_type=F32)
    for hd in range(N_HEADS):
        vt_ref[0, hd] = vt[V_HEAD_DIM * hd: V_HEAD_DIM * (hd + 1), :].astype(BF16)


def _in_proj(x2, mod3, g_pre, w_in_r, q_norm_g, w_uq_r, kv_norm_g, w_uk, w_uvt, pos128, invf128,
             batch, seq, tm):
    nt = seq // tm
    q_scale = (1.0 / math.sqrt(QK_NOPE_DIM + QK_ROPE_DIM)) * math.log2(math.e)
    tok = lambda i: (i, 0)
    heads = lambda i: (i // nt, 0, i % nt, 0)
    return pl.pallas_call(
        functools.partial(_in_proj_kernel, q_scale=q_scale),
        grid=(batch * nt,),
        in_specs=[
            pl.BlockSpec((tm, D_MODEL), tok),
            pl.BlockSpec((1, 6, D_MODEL), lambda i: (i // nt, 0, 0)),
            _const_spec((1, D_MODEL)),
            _const_spec(w_in_r.shape),
            _const_spec((1, Q_LORA_RANK)),
            _const_spec(w_uq_r.shape),
            _const_spec((1, KV_LORA_RANK)),
            _const_spec(w_uk.shape),
            _const_spec(w_uvt.shape),
            pl.BlockSpec((tm, LANES), tok),
            _const_spec((1, LANES)),
        ],
        out_specs=[
            pl.BlockSpec((tm, CONV_DIM), tok),
            pl.BlockSpec((tm, 2 * D_MODEL), tok),
            pl.BlockSpec((1, N_HEADS, tm, QK_PAD_DIM), heads),
            pl.BlockSpec((1, N_HEADS, tm, QK_PAD_DIM), heads),
            pl.BlockSpec((1, N_HEADS, V_HEAD_DIM, tm), lambda i: (i // nt, 0, 0, i % nt)),
        ],
        out_shape=[
            jax.ShapeDtypeStruct((batch * seq, CONV_DIM), BF16),
            jax.ShapeDtypeStruct((batch * seq, 2 * D_MODEL), BF16),
            jax.ShapeDtypeStruct((batch, N_HEADS, seq, QK_PAD_DIM), BF16),
            jax.ShapeDtypeStruct((batch, N_HEADS, seq, QK_PAD_DIM), BF16),
            jax.ShapeDtypeStruct((batch, N_HEADS, V_HEAD_DIM, seq), BF16),
        ],
        compiler_params=_params("arbitrary"),
        name="in_proj",
    )(x2, mod3, g_pre, w_in_r, q_norm_g, w_uq_r, kv_norm_g, w_uk, w_uvt, pos128, invf128)


def _conv_kernel(u_ref, halo_ref, w_ref, b_ref, lng_ref, lnb_ref, wout_ref, o_ref, buf_ref, act_ref, *,
                 ts, rows):
    t = pl.program_id(1)
    halo = halo_ref[0].astype(F32)
    buf_ref[0:CONV_HALO, :] = jnp.where(t > 0, halo, 0.0)
    buf_ref[CONV_HALO:, :] = u_ref[0].astype(F32)
    first = CONV_HALO - (CONV_KERNEL - 1)
    for r0 in range(0, ts, rows):
        acc = jnp.broadcast_to(b_ref[...], (rows, CONV_DIM))
        for k in range(CONV_KERNEL):
            acc = acc + w_ref[k:k + 1, :] * buf_ref[r0 + first + k: r0 + first + k + rows, :]
        mu = jnp.mean(acc, axis=-1, keepdims=True)
        cen = acc - mu
        var = jnp.mean(cen * cen, axis=-1, keepdims=True)
        y = cen * lax.rsqrt(var + EPS) * lng_ref[...] + lnb_ref[...]
        y = y * jax.nn.sigmoid(y)
        act_ref[r0:r0 + rows, :] = y.astype(BF16)
    o_ref[0] = _dot(act_ref[...], wout_ref[...]).astype(BF16)


def _conv(u3, conv_w, conv_b, ln_g, ln_b, w_conv_out, ts, rows):
    batch, seq, _ = u3.shape
    hb = ts // CONV_HALO
    return pl.pallas_call(
        functools.partial(_conv_kernel, ts=ts, rows=rows),
        grid=(batch, seq // ts),
        in_specs=[
            pl.BlockSpec((1, ts, CONV_DIM), lambda b, t: (b, t, 0)),
            pl.BlockSpec((1, CONV_HALO, CONV_DIM), lambda b, t: (b, jnp.maximum(t * hb - 1, 0), 0)),
            _const_spec(conv_w.shape),
            _const_spec((1, CONV_DIM)),
            _const_spec((1, CONV_DIM)),
            _const_spec((1, CONV_DIM)),
            _const_spec(w_conv_out.shape),
        ],
        out_specs=pl.BlockSpec((1, ts, D_MODEL), lambda b, t: (b, t, 0)),
        out_shape=jax.ShapeDtypeStruct((batch, seq, D_MODEL), BF16),
        scratch_shapes=[pltpu.VMEM((CONV_HALO + ts, CONV_DIM), F32), pltpu.VMEM((ts, CONV_DIM), BF16)],
        compiler_params=_params("arbitrary", "arbitrary"),
        name="conv",
    )(u3, u3, conv_w, conv_b, ln_g, ln_b, w_conv_out)


def _attn_kernel(q_ref, k_ref, vt_ref, o_ref, m_ref, l_ref, acc_ref, s_ref, *, tile):
    qi = pl.program_id(2)
    m_ref[...] = jnp.full(m_ref.shape, -1e30, F32)
    l_ref[...] = jnp.zeros(l_ref.shape, F32)
    acc_ref[...] = jnp.zeros(acc_ref.shape, F32)

    def scores(hd, j):
        start = pl.multiple_of(j * tile, tile)
        k = k_ref[0, hd, pl.ds(start, tile), :]
        s_ref[hd] = lax.dot_general(k, q_ref[0, hd], (((1,), (1,)), ((), ())),
                                    preferred_element_type=F32)

    def update(hd, j, masked):
        start = pl.multiple_of(j * tile, tile)
        s = s_ref[hd]
        if masked:
            kpos = lax.broadcasted_iota(jnp.int32, (tile, tile), 0)
            qpos = lax.broadcasted_iota(jnp.int32, (tile, tile), 1)
            s = jnp.where(qpos >= kpos, s, -1e30)
        m_old = m_ref[hd]
        m_new = jnp.maximum(m_old, jnp.max(s, axis=0, keepdims=True))
        alpha = jnp.exp2(m_old - m_new)
        p = jnp.exp2(s - m_new)
        m_ref[hd] = m_new
        l_ref[hd] = alpha * l_ref[hd] + jnp.sum(p, axis=0, keepdims=True)
        vt = vt_ref[0, hd, :, pl.ds(start, tile)]
        acc_ref[hd] = alpha * acc_ref[hd] + _dot(vt, p.astype(BF16))

    scores(0, 0)

    def body(j, carry):
        scores(1, j)
        update(0, j, False)
        scores(0, j + 1)
        update(1, j, False)
        return carry

    lax.fori_loop(0, qi, body, 0)
    scores(1, qi)
    update(0, qi, True)
    update(1, qi, True)
    for hd in range(2):
        o = acc_ref[hd] / l_ref[hd]
        o_ref[0, :, V_HEAD_DIM * hd: V_HEAD_DIM * (hd + 1)] = o.T.astype(BF16)


def _attn(q4, k4, vt4, tile):
    batch, heads, seq, _ = q4.shape
    hpb = 2
    tq = tile
    return pl.pallas_call(
        functools.partial(_attn_kernel, tile=tile),
        grid=(batch, heads // hpb, seq // tq),
        in_specs=[
            pl.BlockSpec((1, hpb, tq, QK_PAD_DIM), lambda b, h, i: (b, h, i, 0)),
            pl.BlockSpec((1, hpb, seq, QK_PAD_DIM), lambda b, h, i: (b, h, 0, 0)),
            pl.BlockSpec((1, hpb, V_HEAD_DIM, seq), lambda b, h, i: (b, h, 0, 0)),
        ],
        out_specs=pl.BlockSpec((1, tq, hpb * V_HEAD_DIM), lambda b, h, i: (b, i, h)),
        out_shape=jax.ShapeDtypeStruct((batch, seq, heads * V_HEAD_DIM), BF16),
        scratch_shapes=[pltpu.VMEM((hpb, 1, tq), F32), pltpu.VMEM((hpb, 1, tq), F32),
                        pltpu.VMEM((hpb, V_HEAD_DIM, tq), F32), pltpu.VMEM((hpb, tile, tile), F32)],
        compiler_params=_params("arbitrary", "arbitrary", "arbitrary"),
        name="attn",
    )(q4, k4, vt4)


def _mix_out_kernel(o_ref, ya_ref, gate_ref, x_ref, mod_ref, gpost_ref, wao_ref, wout_ref, x1_ref):
    y_b = _dot(o_ref[...], wao_ref[...])
    g_a = gate_ref[:, :D_MODEL].astype(F32)
    g_b = gate_ref[:, D_MODEL:].astype(F32)
    merged = g_a * ya_ref[...].astype(F32) + g_b * y_b
    y = _dot(merged.astype(BF16), wout_ref[...])
    gate1 = mod_ref[0, 2:3, :]
    x1_ref[...] = x_ref[...] + gate1 * (_rms(y) * gpost_ref[...])


def _mix_out(o2, ya2, gates, x2, mod3, g_post, w_attn_out, w_out, seq, tm):
    tokens = x2.shape[0]
    nt = seq // tm
    tok = lambda i: (i, 0)
    return pl.pallas_call(
        _mix_out_kernel,
        grid=(tokens // tm,),
        in_specs=[
            pl.BlockSpec((tm, D_MODEL), tok),
            pl.BlockSpec((tm, D_MODEL), tok),
            pl.BlockSpec((tm, 2 * D_MODEL), tok),
            pl.BlockSpec((tm, D_MODEL), tok),
            pl.BlockSpec((1, 6, D_MODEL), lambda i: (i // nt, 0, 0)),
            _const_spec((1, D_MODEL)),
            _const_spec(w_attn_out.shape),
            _const_spec(w_out.shape),
        ],
        out_specs=pl.BlockSpec((tm, D_MODEL), tok),
        out_shape=jax.ShapeDtypeStruct((tokens, D_MODEL), F32),
        compiler_params=_params("arbitrary"),
        name="mix_out",
    )(o2, ya2, gates, x2, mod3, g_post, w_attn_out, w_out)


def _mlp_kernel(x_ref, mod_ref, gpre_ref, gpost_ref, w1_ref, w2_ref, o_ref):
    x = x_ref[...]
    shift = mod_ref[0, 3:4, :]
    scale = mod_ref[0, 4:5, :]
    gate = mod_ref[0, 5:6, :]
    h = (_rms(x) * gpre_ref[...]) * (1.0 + scale) + shift
    hid = jnp.maximum(_dot(h.astype(BF16), w1_ref[...]), 0.0)
    y = _dot((hid * hid).astype(BF16), w2_ref[...])
    o_ref[...] = x + gate * (_rms(y) * gpost_ref[...])


def _mlp(x1, mod3, g_pre, g_post, w1, w2, seq, tm):
    tokens = x1.shape[0]
    nt = seq // tm
    tok = lambda i: (i, 0)
    return pl.pallas_call(
        _mlp_kernel,
        grid=(tokens // tm,),
        in_specs=[
            pl.BlockSpec((tm, D_MODEL), tok),
            pl.BlockSpec((1, 6, D_MODEL), lambda i: (i // nt, 0, 0)),
            _const_spec((1, D_MODEL)),
            _const_spec((1, D_MODEL)),
            _const_spec(w1.shape),
            _const_spec(w2.shape),
        ],
        out_specs=pl.BlockSpec((tm, D_MODEL), tok),
        out_shape=jax.ShapeDtypeStruct((tokens, D_MODEL), F32),
        compiler_params=_params("arbitrary"),
        name="mlp",
    )(x1, mod3, g_pre, g_post, w1, w2)


def _head_major_cols(w, split):
    k = w.shape[0]
    w3 = w.reshape(k, N_HEADS, -1)
    return jnp.concatenate([w3[:, :, :split].reshape(k, -1), w3[:, :, split:].reshape(k, -1)], axis=1)


def kernel(x, c, positions, w_ada, b_ada, g_pre_mix, g_post_mix, g_pre_mlp, g_post_mlp, w_in, conv_w, conv_b, conv_norm_g, conv_norm_b, w_conv_out, q_norm_g, w_uq, kv_norm_g, w_ukv, w_attn_out, w_out, w_mlp_in, w_mlp_out):
    batch, seq, d = x.shape
    depth = w_ada.shape[0]
    tokens = batch * seq
    row = lambda v: v.reshape(1, -1)

    inv_freq = 1.0 / (ROPE_THETA ** (jnp.arange(0, QK_ROPE_DIM, 2, dtype=F32) / QK_ROPE_DIM))
    invf128 = jnp.tile(inv_freq, LANES // (QK_ROPE_DIM // 2)).reshape(1, LANES)
    pos128 = jnp.broadcast_to(positions.astype(F32).reshape(tokens, 1), (tokens, LANES))
    c_pad = jnp.pad(c, ((0, 8 - batch), (0, 0)))

    x2 = x.reshape(tokens, d)
    for l in range(depth):
        wi = w_in[l]
        s0, s1, s2, s3 = 2 * CONV_DIM, 2 * CONV_DIM + Q_LORA_RANK, 2 * CONV_DIM + Q_LORA_RANK + KV_LORA_RANK, \
            2 * CONV_DIM + Q_LORA_RANK + KV_LORA_RANK + QK_ROPE_DIM
        w_in_r = jnp.concatenate([wi[:, :s0], wi[:, s3:], wi[:, s0:s1], wi[:, s1:s2], wi[:, s2:s3], wi[:, s2:s3]],
                                 axis=1).astype(BF16)
        w_uq_r = _head_major_cols(w_uq[l], QK_NOPE_DIM).astype(BF16)
        w_ukv_r = _head_major_cols(w_ukv[l], QK_NOPE_DIM).astype(BF16)
        w_uk = w_ukv_r[:, :N_HEADS * QK_NOPE_DIM]
        w_uvt = w_ukv_r[:, N_HEADS * QK_NOPE_DIM:].T

        mod = _ada(c_pad, w_ada[l], row(b_ada[l]))
        mod3 = mod[:batch].reshape(batch, 6, d)

        u, gates, q4, k4, vt4 = _in_proj(x2, mod3, row(g_pre_mix[l]), w_in_r, row(q_norm_g[l]), w_uq_r,
                                         row(kv_norm_g[l]), w_uk, w_uvt, pos128, invf128, batch, seq, tm=256)
        y_a = _conv(u.reshape(batch, seq, CONV_DIM), conv_w[l], row(conv_b[l]), row(conv_norm_g[l]),
                    row(conv_norm_b[l]), w_conv_out[l].astype(BF16), ts=512, rows=32)
        o = _attn(q4, k4, vt4, tile=512)
        x2 = _mix_out(o.reshape(tokens, d), y_a.reshape(tokens, d), gates, x2, mod3, row(g_post_mix[l]),
                      w_attn_out[l].astype(BF16), w_out[l].astype(BF16), seq, tm=512)
        x2 = _mlp(x2, mod3, row(g_pre_mlp[l]), row(g_post_mlp[l]), w_mlp_in[l].astype(BF16),
                  w_mlp_out[l].astype(BF16), seq, tm=512)
    return x2.reshape(batch, seq, d)
```

```python
import functools
import math

import jax
import jax.numpy as jnp
import numpy as np
from jax import lax
from jax.experimental import pallas as pl
from jax.experimental.pallas import tpu as pltpu

D_MODEL = 1024
CONV_DIM = 1024
CONV_KERNEL = 31
N_HEADS = 8
QK_NOPE_DIM = 128
QK_ROPE_DIM = 64
V_HEAD_DIM = 128
Q_LORA_RANK = 384
KV_LORA_RANK = 256
ROPE_THETA = 10000.0
D_FF = 4 * D_MODEL
EPS = 1e-6

LANES = 128
SUBLANES = 8
QK_PAD_DIM = 2 * LANES
CONV_HALO = 32
VMEM_LIMIT = 56 * 1024 * 1024

_C_GLU = 0
_C_GATE = 2 * CONV_DIM
_C_QLAT = _C_GATE + 2 * D_MODEL
_C_KVLAT = _C_QLAT + Q_LORA_RANK
_C_KR = _C_KVLAT + KV_LORA_RANK
_C_END = _C_KR + LANES

BF16 = jnp.bfloat16
F32 = jnp.float32


def _dot(a, b):
    return jnp.dot(a, b, preferred_element_type=F32)


def _rms(v):
    return v * lax.rsqrt(jnp.mean(v * v, axis=-1, keepdims=True) + EPS)


def _params(*sem):
    return pltpu.CompilerParams(dimension_semantics=sem, vmem_limit_bytes=VMEM_LIMIT)


def _const_spec(shape):
    nd = len(shape)
    return pl.BlockSpec(shape, lambda *_: (0,) * nd, pipeline_mode=pl.Buffered(1))


def _ada_kernel(c_ref, w_ref, b_ref, o_ref):
    c = c_ref[...]
    act = c * jax.nn.sigmoid(c)
    o_ref[...] = _dot(act.astype(BF16), w_ref[...].astype(BF16)) + b_ref[...]


def _ada(c_pad, w_ada, b_ada):
    rows, d = c_pad.shape
    n = w_ada.shape[1]
    tn = 1024
    return pl.pallas_call(
        _ada_kernel,
        grid=(n // tn,),
        in_specs=[
            pl.BlockSpec((rows, d), lambda j: (0, 0)),
            pl.BlockSpec((d, tn), lambda j: (0, j)),
            pl.BlockSpec((1, tn), lambda j: (0, j)),
        ],
        out_specs=pl.BlockSpec((rows, tn), lambda j: (0, j)),
        out_shape=jax.ShapeDtypeStruct((rows, n), F32),
        compiler_params=_params("arbitrary"),
        name="ada",
    )(c_pad, w_ada, b_ada)


def _rope_pairs(t, cos, sin, first_half):
    rot = jnp.where(first_half, -pltpu.roll(t, LANES - QK_ROPE_DIM // 2, 1),
                    pltpu.roll(t, QK_ROPE_DIM // 2, 1))
    return t * cos + rot * sin


def _in_proj_kernel(x_ref, mod_ref, gpre_ref, win_ref, qg_ref, wuq_ref, kvg_ref, wuk_ref, wuvt_ref,
                    pos_ref, invf_ref, u_ref, gate_ref, q_ref, k_ref, vt_ref, *, q_scale):
    x = x_ref[...]
    shift = mod_ref[0, 0:1, :]
    scale = mod_ref[0, 1:2, :]
    h = (_rms(x) * gpre_ref[...]) * (1.0 + scale) + shift
    hb = h.astype(BF16)

    zg = _dot(hb, win_ref[:, _C_GLU:_C_GATE])
    u_ref[...] = (zg[:, :CONV_DIM] * jax.nn.sigmoid(zg[:, CONV_DIM:])).astype(BF16)

    gate_ref[...] = jax.nn.sigmoid(_dot(hb, win_ref[:, _C_GATE:_C_QLAT])).astype(BF16)

    ang = pos_ref[...] * invf_ref[...]
    cos = jnp.cos(ang)
    sin = jnp.sin(ang)
    lane = lax.broadcasted_iota(jnp.int32, ang.shape, 1)
    first_half = (lane % QK_ROPE_DIM) < (QK_ROPE_DIM // 2)
    low = lane < QK_ROPE_DIM

    ql = _dot(hb, win_ref[:, _C_QLAT:_C_KVLAT])
    qn = (_rms(ql) * qg_ref[...]).astype(BF16)
    q = _dot(qn, wuq_ref[...]) * q_scale
    nope_w = N_HEADS * QK_NOPE_DIM
    for hp in range(N_HEADS // 2):
        qr = _rope_pairs(q[:, nope_w + LANES * hp: nope_w + LANES * (hp + 1)], cos, sin, first_half)
        qr_hi = pltpu.roll(qr, QK_ROPE_DIM, 1)
        for j, part in ((0, qr), (1, qr_hi)):
            hd = 2 * hp + j
            q_ref[0, hd, :, 0:LANES] = q[:, LANES * hd: LANES * (hd + 1)].astype(BF16)
            q_ref[0, hd, :, LANES:QK_PAD_DIM] = jnp.where(low, part, 0.0).astype(BF16)

    kvl = _dot(hb, win_ref[:, _C_KVLAT:_C_KR])
    kvn = (_rms(kvl) * kvg_ref[...]).astype(BF16)
    kn = _dot(kvn, wuk_ref[...])
    kr = _rope_pairs(_dot(hb, win_ref[:, _C_KR:_C_END]), cos, sin, first_half)
    kr = jnp.where(low, kr, 0.0).astype(BF16)
    for hd in range(N_HEADS):
        k_ref[0, hd, :, 0:LANES] = kn[:, LANES * hd: LANES * (hd + 1)].astype(BF16)
        k_ref[0, hd, :, LANES:QK_PAD_DIM] = kr
    vt = lax.dot_general(wuvt_ref[...], kvn, (((1,), (1,)), ((), ())), preferred_element_type=F32)
    for hd in range(N_HEADS):
        vt_ref[0, hd] = vt[V_HEAD_DIM * hd: V_HEAD_DIM * (hd + 1), :].astype(BF16)


def _in_proj(x2, mod3, g_pre, w_in_r, q_norm_g, w_uq_r, kv_norm_g, w_uk, w_uvt, pos128, invf128,
             batch, seq, tm):
    nt = seq // tm
    q_scale = (1.0 / math.sqrt(QK_NOPE_DIM + QK_ROPE_DIM)) * math.log2(math.e)
    tok = lambda i: (i, 0)
    heads = lambda i: (i // nt, 0, i % nt, 0)
    return pl.pallas_call(
        functools.partial(_in_proj_kernel, q_scale=q_scale),
        grid=(batch * nt,),
        in_specs=[
            pl.BlockSpec((tm, D_MODEL), tok),
            pl.BlockSpec((1, 6, D_MODEL), lambda i: (i // nt, 0, 0)),
            _const_spec((1, D_MODEL)),
            _const_spec(w_in_r.shape),
            _const_spec((1, Q_LORA_RANK)),
            _const_spec(w_uq_r.shape),
            _const_spec((1, KV_LORA_RANK)),
            _const_spec(w_uk.shape),
            _const_spec(w_uvt.shape),
            pl.BlockSpec((tm, LANES), tok),
            _const_spec((1, LANES)),
        ],
        out_specs=[
            pl.BlockSpec((tm, CONV_DIM), tok),
            pl.BlockSpec((tm, 2 * D_MODEL), tok),
            pl.BlockSpec((1, N_HEADS, tm, QK_PAD_DIM), heads),
            pl.BlockSpec((1, N_HEADS, tm, QK_PAD_DIM), heads),
            pl.BlockSpec((1, N_HEADS, V_HEAD_DIM, tm), lambda i: (i // nt, 0, 0, i % nt)),
        ],
        out_shape=[
            jax.ShapeDtypeStruct((batch * seq, CONV_DIM), BF16),
            jax.ShapeDtypeStruct((batch * seq, 2 * D_MODEL), BF16),
            jax.ShapeDtypeStruct((batch, N_HEADS, seq, QK_PAD_DIM), BF16),
            jax.ShapeDtypeStruct((batch, N_HEADS, seq, QK_PAD_DIM), BF16),
            jax.ShapeDtypeStruct((batch, N_HEADS, V_HEAD_DIM, seq), BF16),
        ],
        compiler_params=_params("arbitrary"),
        name="in_proj",
    )(x2, mod3, g_pre, w_in_r, q_norm_g, w_uq_r, kv_norm_g, w_uk, w_uvt, pos128, invf128)


def _conv_kernel(u_ref, halo_ref, w_ref, b_ref, lng_ref, lnb_ref, wout_ref, o_ref, buf_ref, act_ref, *,
                 ts, rows):
    t = pl.program_id(1)
    halo = halo_ref[0].astype(F32)
    buf_ref[0, 0:CONV_HALO, :] = jnp.where(t > 0, halo, 0.0)
    buf_ref[0, CONV_HALO:, :] = u_ref[0].astype(F32)
    n_shift = ts + CONV_HALO - SUBLANES
    for b in range(1, SUBLANES):
        buf_ref[b, 0:n_shift, :] = buf_ref[0, b:b + n_shift, :]
    first = CONV_HALO - (CONV_KERNEL - 1)

    groups = rows // SUBLANES
    half = CONV_DIM // 2

    def chunk(i, carry):
        r0 = pl.multiple_of(i * rows, rows)
        halves = []
        for c0 in (0, half):
            acc = jnp.broadcast_to(b_ref[:, c0:c0 + half][None], (groups, SUBLANES, half))
            for k in range(CONV_KERNEL):
                a, b = divmod(first + k, SUBLANES)
                win = buf_ref[b, pl.ds(r0 + SUBLANES * a, rows), c0:c0 + half]
                acc = acc + w_ref[k, :, c0:c0 + half][None] * win.reshape(groups, SUBLANES, half)
            halves.append(acc.reshape(rows, half))
        acc = jnp.concatenate(halves, axis=-1)
        mu = jnp.mean(acc, axis=-1, keepdims=True)
        cen = acc - mu
        var = jnp.mean(cen * cen, axis=-1, keepdims=True)
        y = cen * lax.rsqrt(var + EPS) * lng_ref[...] + lnb_ref[...]
        y = y * jax.nn.sigmoid(y)
        act_ref[pl.ds(r0, rows), :] = y.astype(BF16)
        return carry

    lax.fori_loop(0, ts // rows, chunk, 0, unroll=4)
    o_ref[0] = _dot(act_ref[...], wout_ref[...]).astype(BF16)


def _conv(u3, conv_w, conv_b, ln_g, ln_b, w_conv_out, ts, rows):
    batch, seq, _ = u3.shape
    hb = ts // CONV_HALO
    return pl.pallas_call(
        functools.partial(_conv_kernel, ts=ts, rows=rows),
        grid=(batch, seq // ts),
        in_specs=[
            pl.BlockSpec((1, ts, CONV_DIM), lambda b, t: (b, t, 0)),
            pl.BlockSpec((1, CONV_HALO, CONV_DIM), lambda b, t: (b, jnp.maximum(t * hb - 1, 0), 0)),
            _const_spec(conv_w.shape),
            _const_spec((SUBLANES, CONV_DIM)),
            _const_spec((1, CONV_DIM)),
            _const_spec((1, CONV_DIM)),
            _const_spec(w_conv_out.shape),
        ],
        out_specs=pl.BlockSpec((1, ts, D_MODEL), lambda b, t: (b, t, 0)),
        out_shape=jax.ShapeDtypeStruct((batch, seq, D_MODEL), BF16),
        scratch_shapes=[pltpu.VMEM((SUBLANES, CONV_HALO + ts, CONV_DIM), F32),
                        pltpu.VMEM((ts, CONV_DIM), BF16)],
        compiler_params=_params("arbitrary", "arbitrary"),
        name="conv",
    )(u3, u3, conv_w, conv_b, ln_g, ln_b, w_conv_out)


def _attn_kernel(q_ref, k_ref, vt_ref, o_ref, m_ref, l_ref, acc_ref, s_ref, *, tq, tk_full, tk_diag):
    qi = pl.program_id(2)
    n_full = qi * (tq // tk_full)
    n_diag = tq // tk_diag
    diag_start = qi * tq
    m_ref[...] = jnp.full(m_ref.shape, -1e30, F32)
    l_ref[...] = jnp.zeros(l_ref.shape, F32)
    acc_ref[...] = jnp.zeros(acc_ref.shape, F32)

    def scores(hd, start, tk, q0=0):
        k = k_ref[0, hd, pl.ds(start, tk), :]
        s_ref[hd, :tk, q0:] = lax.dot_general(k, q_ref[0, hd, q0:, :], (((1,), (1,)), ((), ())),
                                              preferred_element_type=F32)

    def update(hd, start, tk, q0=0, diag=False):
        s = s_ref[hd, :tk, q0:]
        if diag:
            kpos = lax.broadcasted_iota(jnp.int32, (tk, tk), 0)
            qpos = lax.broadcasted_iota(jnp.int32, (tk, tk), 1)
            tri = jnp.where(qpos >= kpos, s[:, :tk], -1e30)
            s = tri if q0 + tk == tq else jnp.concatenate([tri, s[:, tk:]], axis=1)
        m_old = m_ref[hd, :, q0:]
        m_new = jnp.maximum(m_old, jnp.max(s, axis=0, keepdims=True))
        alpha = jnp.exp2(m_old - m_new)
        p = jnp.exp2(s - m_new)
        m_ref[hd, :, q0:] = m_new
        l_ref[hd, :, q0:] = alpha * l_ref[hd, :, q0:] + jnp.sum(p, axis=0, keepdims=True)
        vt = vt_ref[0, hd, :, pl.ds(start, tk)]
        acc_ref[hd, :, q0:] = alpha * acc_ref[hd, :, q0:] + _dot(vt, p.astype(BF16))

    def full_start(j):
        return pl.multiple_of(j * tk_full, tk_full)

    def diag_tile(d):
        return pl.multiple_of(diag_start + d * tk_diag, tk_diag), tk_diag, d * tk_diag

    if tk_full == tk_diag:
        scores(0, full_start(0), tk_full)
    else:
        pl.when(n_full > 0)(lambda: scores(0, full_start(0), tk_full))
        pl.when(n_full == 0)(lambda: scores(0, *diag_tile(0)))

    def full_pair(j, next_scores):
        scores(1, full_start(j), tk_full)
        update(0, full_start(j), tk_full)
        next_scores()
        update(1, full_start(j), tk_full)

    def body(j, carry):
        full_pair(j, lambda: scores(0, full_start(j + 1), tk_full))
        return carry

    lax.fori_loop(0, jnp.maximum(n_full - 1, 0), body, 0)
    pl.when(n_full > 0)(lambda: full_pair(n_full - 1, lambda: scores(0, *diag_tile(0))))
    for d in range(n_diag):
        scores(1, *diag_tile(d))
        update(0, *diag_tile(d), True)
        if d + 1 < n_diag:
            scores(0, *diag_tile(d + 1))
        update(1, *diag_tile(d), True)
    for hd in range(2):
        o = acc_ref[hd] / l_ref[hd]
        o_ref[0, :, V_HEAD_DIM * hd: V_HEAD_DIM * (hd + 1)] = o.T.astype(BF16)


def _attn(q4, k4, vt4, tq, tk_full, tk_diag):
    batch, heads, seq, _ = q4.shape
    hpb = 2
    return pl.pallas_call(
        functools.partial(_attn_kernel, tq=tq, tk_full=tk_full, tk_diag=tk_diag),
        grid=(batch, heads // hpb, seq // tq),
        in_specs=[
            pl.BlockSpec((1, hpb, tq, QK_PAD_DIM), lambda b, h, i: (b, h, i, 0)),
            pl.BlockSpec((1, hpb, seq, QK_PAD_DIM), lambda b, h, i: (b, h, 0, 0)),
            pl.BlockSpec((1, hpb, V_HEAD_DIM, seq), lambda b, h, i: (b, h, 0, 0)),
        ],
        out_specs=pl.BlockSpec((1, tq, hpb * V_HEAD_DIM), lambda b, h, i: (b, i, h)),
        out_shape=jax.ShapeDtypeStruct((batch, seq, heads * V_HEAD_DIM), BF16),
        scratch_shapes=[pltpu.VMEM((hpb, 1, tq), F32), pltpu.VMEM((hpb, 1, tq), F32),
                        pltpu.VMEM((hpb, V_HEAD_DIM, tq), F32),
                        pltpu.VMEM((hpb, max(tk_full, tk_diag), tq), F32)],
        compiler_params=_params("arbitrary", "arbitrary", "arbitrary"),
        name="attn",
    )(q4, k4, vt4)


def _mix_out_kernel(o_ref, ya_ref, gate_ref, x_ref, mod_ref, gpost_ref, wao_ref, wout_ref, x1_ref):
    y_b = _dot(o_ref[...], wao_ref[...])
    g_a = gate_ref[:, :D_MODEL].astype(F32)
    g_b = gate_ref[:, D_MODEL:].astype(F32)
    merged = g_a * ya_ref[...].astype(F32) + g_b * y_b
    y = _dot(merged.astype(BF16), wout_ref[...])
    gate1 = mod_ref[0, 2:3, :]
    x1_ref[...] = x_ref[...] + gate1 * (_rms(y) * gpost_ref[...])


def _mix_out(o2, ya2, gates, x2, mod3, g_post, w_attn_out, w_out, seq, tm):
    tokens = x2.shape[0]
    nt = seq // tm
    tok = lambda i: (i, 0)
    return pl.pallas_call(
        _mix_out_kernel,
        grid=(tokens // tm,),
        in_specs=[
            pl.BlockSpec((tm, D_MODEL), tok),
            pl.BlockSpec((tm, D_MODEL), tok),
            pl.BlockSpec((tm, 2 * D_MODEL), tok),
            pl.BlockSpec((tm, D_MODEL), tok),
            pl.BlockSpec((1, 6, D_MODEL), lambda i: (i // nt, 0, 0)),
            _const_spec((1, D_MODEL)),
            _const_spec(w_attn_out.shape),
            _const_spec(w_out.shape),
        ],
        out_specs=pl.BlockSpec((tm, D_MODEL), tok),
        out_shape=jax.ShapeDtypeStruct((tokens, D_MODEL), F32),
        compiler_params=_params("arbitrary"),
        name="mix_out",
    )(o2, ya2, gates, x2, mod3, g_post, w_attn_out, w_out)


def _mlp_kernel(x_ref, mod_ref, gpre_ref, gpost_ref, w1_ref, w2_ref, o_ref):
    x = x_ref[...]
    shift = mod_ref[0, 3:4, :]
    scale = mod_ref[0, 4:5, :]
    gate = mod_ref[0, 5:6, :]
    h = (_rms(x) * gpre_ref[...]) * (1.0 + scale) + shift
    hid = jnp.maximum(_dot(h.astype(BF16), w1_ref[...]), 0.0)
    y = _dot((hid * hid).astype(BF16), w2_ref[...])
    o_ref[...] = x + gate * (_rms(y) * gpost_ref[...])


def _mlp(x1, mod3, g_pre, g_post, w1, w2, seq, tm):
    tokens = x1.shape[0]
    nt = seq // tm
    tok = lambda i: (i, 0)
    return pl.pallas_call(
        _mlp_kernel,
        grid=(tokens // tm,),
        in_specs=[
            pl.BlockSpec((tm, D_MODEL), tok),
            pl.BlockSpec((1, 6, D_MODEL), lambda i: (i // nt, 0, 0)),
            _const_spec((1, D_MODEL)),
            _const_spec((1, D_MODEL)),
            _const_spec(w1.shape),
            _const_spec(w2.shape),
        ],
        out_specs=pl.BlockSpec((tm, D_MODEL), tok),
        out_shape=jax.ShapeDtypeStruct((tokens, D_MODEL), F32),
        compiler_params=_params("arbitrary"),
        name="mlp",
    )(x1, mod3, g_pre, g_post, w1, w2)


def _head_major_cols(w, split):
    k = w.shape[0]
    w3 = w.reshape(k, N_HEADS, -1)
    return jnp.concatenate([w3[:, :, :split].reshape(k, -1), w3[:, :, split:].reshape(k, -1)], axis=1)


def kernel(x, c, positions, w_ada, b_ada, g_pre_mix, g_post_mix, g_pre_mlp, g_post_mlp, w_in, conv_w, conv_b, conv_norm_g, conv_norm_b, w_conv_out, q_norm_g, w_uq, kv_norm_g, w_ukv, w_attn_out, w_out, w_mlp_in, w_mlp_out):
    batch, seq, d = x.shape
    depth = w_ada.shape[0]
    tokens = batch * seq
    row = lambda v: v.reshape(1, -1)

    inv_freq = 1.0 / (ROPE_THETA ** (jnp.arange(0, QK_ROPE_DIM, 2, dtype=F32) / QK_ROPE_DIM))
    invf128 = jnp.tile(inv_freq, LANES // (QK_ROPE_DIM // 2)).reshape(1, LANES)
    pos128 = jnp.broadcast_to(positions.astype(F32).reshape(tokens, 1), (tokens, LANES))
    c_pad = jnp.pad(c, ((0, 8 - batch), (0, 0)))

    x2 = x.reshape(tokens, d)
    for l in range(depth):
        wi = w_in[l].astype(BF16)
        s0, s1, s2, s3 = 2 * CONV_DIM, 2 * CONV_DIM + Q_LORA_RANK, 2 * CONV_DIM + Q_LORA_RANK + KV_LORA_RANK, \
            2 * CONV_DIM + Q_LORA_RANK + KV_LORA_RANK + QK_ROPE_DIM
        w_in_r = jnp.concatenate([wi[:, :s0], wi[:, s3:], wi[:, s0:s1], wi[:, s1:s2], wi[:, s2:s3], wi[:, s2:s3]],
                                 axis=1)
        w_uq_r = _head_major_cols(w_uq[l], QK_NOPE_DIM).astype(BF16)
        w_ukv_r = _head_major_cols(w_ukv[l], QK_NOPE_DIM).astype(BF16)
        w_uk = w_ukv_r[:, :N_HEADS * QK_NOPE_DIM]
        w_uvt = w_ukv_r[:, N_HEADS * QK_NOPE_DIM:].T

        mod = _ada(c_pad, w_ada[l], row(b_ada[l]))
        mod3 = mod[:batch].reshape(batch, 6, d)

        u, gates, q4, k4, vt4 = _in_proj(x2, mod3, row(g_pre_mix[l]), w_in_r, row(q_norm_g[l]), w_uq_r,
                                         row(kv_norm_g[l]), w_uk, w_uvt, pos128, invf128, batch, seq, tm=256)
        conv_w8 = jnp.broadcast_to(conv_w[l][:, None, :], (CONV_KERNEL, SUBLANES, CONV_DIM))
        conv_b8 = jnp.broadcast_to(row(conv_b[l]), (SUBLANES, CONV_DIM))
        y_a = _conv(u.reshape(batch, seq, CONV_DIM), conv_w8, conv_b8, row(conv_norm_g[l]),
                    row(conv_norm_b[l]), w_conv_out[l].astype(BF16), ts=512, rows=32)
        o = _attn(q4, k4, vt4, tq=1024, tk_full=512, tk_diag=512)
        x2 = _mix_out(o.reshape(tokens, d), y_a.reshape(tokens, d), gates, x2, mod3, row(g_post_mix[l]),
                      w_attn_out[l].astype(BF16), w_out[l].astype(BF16), seq, tm=512)
        x2 = _mlp(x2, mod3, row(g_pre_mlp[l]), row(g_post_mlp[l]), w_mlp_in[l].astype(BF16),
                  w_mlp_out[l].astype(BF16), seq, tm=512)
    return x2.reshape(batch, seq, d)
```

```python
import functools
import math

import jax
import jax.numpy as jnp
from jax import lax
from jax.experimental import pallas as pl
from jax.experimental.pallas import tpu as pltpu

D_MODEL = 1024
CONV_DIM = 1024
CONV_KERNEL = 31
N_HEADS = 8
QK_NOPE_DIM = 128
QK_ROPE_DIM = 64
V_HEAD_DIM = 128
Q_LORA_RANK = 384
KV_LORA_RANK = 256
ROPE_THETA = 10000.0
D_FF = 4 * D_MODEL
EPS = 1e-6

LANES = 128
SUBLANES = 8
QK_PAD_DIM = 2 * LANES
CONV_HALO = 32
VMEM_LIMIT = 56 * 1024 * 1024

_C_GLU = 0
_C_GATE = 2 * CONV_DIM
_C_QLAT = _C_GATE + 2 * D_MODEL
_C_KVLAT = _C_QLAT + Q_LORA_RANK
_C_KR = _C_KVLAT + KV_LORA_RANK
_C_END = _C_KR + LANES

BF16 = jnp.bfloat16
F32 = jnp.float32


def _dot(a, b):
    return jnp.dot(a, b, preferred_element_type=F32)


def _rms(v):
    return v * lax.rsqrt(jnp.mean(v * v, axis=-1, keepdims=True) + EPS)


def _params(*sem):
    return pltpu.CompilerParams(dimension_semantics=sem, vmem_limit_bytes=VMEM_LIMIT)


def _const_spec(shape):
    nd = len(shape)
    return pl.BlockSpec(shape, lambda *_: (0,) * nd, pipeline_mode=pl.Buffered(1))


def _ada_kernel(c_ref, w_ref, b_ref, o_ref):
    c = c_ref[...]
    act = c * jax.nn.sigmoid(c)
    o_ref[...] = _dot(act.astype(BF16), w_ref[...].astype(BF16)) + b_ref[...]


def _ada(c_pad, w_ada, b_ada):
    rows, d = c_pad.shape
    n = w_ada.shape[1]
    tn = 1024
    return pl.pallas_call(
        _ada_kernel,
        grid=(n // tn,),
        in_specs=[
            pl.BlockSpec((rows, d), lambda j: (0, 0)),
            pl.BlockSpec((d, tn), lambda j: (0, j)),
            pl.BlockSpec((1, tn), lambda j: (0, j)),
        ],
        out_specs=pl.BlockSpec((rows, tn), lambda j: (0, j)),
        out_shape=jax.ShapeDtypeStruct((rows, n), F32),
        compiler_params=_params("arbitrary"),
        name="ada",
    )(c_pad, w_ada, b_ada)


def _rope_pairs(t, cos, sin, first_half):
    rot = jnp.where(first_half, -pltpu.roll(t, LANES - QK_ROPE_DIM // 2, 1),
                    pltpu.roll(t, QK_ROPE_DIM // 2, 1))
    return t * cos + rot * sin


def _in_conv_kernel(x_ref, mod_ref, gpre_ref, win_ref, qg_ref, wuq_ref, kvg_ref, wuk_ref, wuvt_ref,
                    pos_ref, invf_ref, cw_ref, cb_ref, lng_ref, lnb_ref, wco_ref,
                    ya_ref, gate_ref, q_ref, k_ref, vt_ref, buf_ref, act_ref, *, q_scale, tm, nt, rows):
    t = pl.program_id(0) % nt

    @pl.when(t == 0)
    def _():
        buf_ref[0, 0:CONV_HALO, :] = jnp.zeros((CONV_HALO, CONV_DIM), F32)

    x = x_ref[...]
    shift = mod_ref[0, 0:1, :]
    scale = mod_ref[0, 1:2, :]
    h = (_rms(x) * gpre_ref[...]) * (1.0 + scale) + shift
    hb = h.astype(BF16)

    zg = _dot(hb, win_ref[:, _C_GLU:_C_GATE])
    buf_ref[0, CONV_HALO:, :] = zg[:, :CONV_DIM] * jax.nn.sigmoid(zg[:, CONV_DIM:])
    n_shift = tm + CONV_HALO - SUBLANES
    for b in range(1, SUBLANES):
        buf_ref[b, 0:n_shift, :] = buf_ref[0, b:b + n_shift, :]
    first = CONV_HALO - (CONV_KERNEL - 1)
    groups = rows // SUBLANES
    half = CONV_DIM // 2

    def conv_chunk(r0):
        halves = []
        for c0 in (0, half):
            acc = jnp.broadcast_to(cb_ref[:, c0:c0 + half][None], (groups, SUBLANES, half))
            for k in range(CONV_KERNEL):
                a, b = divmod(first + k, SUBLANES)
                win = buf_ref[b, r0 + SUBLANES * a: r0 + SUBLANES * a + rows, c0:c0 + half]
                acc = acc + cw_ref[k, :, c0:c0 + half][None] * win.reshape(groups, SUBLANES, half)
            halves.append(acc.reshape(rows, half))
        acc = jnp.concatenate(halves, axis=-1)
        mu = jnp.mean(acc, axis=-1, keepdims=True)
        cen = acc - mu
        var = jnp.mean(cen * cen, axis=-1, keepdims=True)
        y = cen * lax.rsqrt(var + EPS) * lng_ref[...] + lnb_ref[...]
        y = y * jax.nn.sigmoid(y)
        act_ref[r0:r0 + rows, :] = y.astype(BF16)

    chunk_starts = list(range(0, tm, rows))
    per_stage = -(-len(chunk_starts) // 4)

    def conv_stage(i):
        for r0 in chunk_starts[i * per_stage:(i + 1) * per_stage]:
            conv_chunk(r0)

    gate_ref[...] = jax.nn.sigmoid(_dot(hb, win_ref[:, _C_GATE:_C_QLAT])).astype(BF16)
    conv_stage(0)

    ang_t = invf_ref[...] * pos_ref[0]
    reps = LANES // (QK_ROPE_DIM // 2)
    cos = jnp.concatenate([jnp.cos(ang_t)] * reps, axis=0).T
    sin = jnp.concatenate([jnp.sin(ang_t)] * reps, axis=0).T
    lane = lax.broadcasted_iota(jnp.int32, cos.shape, 1)
    first_half = (lane % QK_ROPE_DIM) < (QK_ROPE_DIM // 2)
    low = lane < QK_ROPE_DIM

    ql = _dot(hb, win_ref[:, _C_QLAT:_C_KVLAT])
    qn = (_rms(ql) * qg_ref[...]).astype(BF16)
    q = _dot(qn, wuq_ref[...]) * q_scale
    nope_w = N_HEADS * QK_NOPE_DIM
    for hp in range(N_HEADS // 2):
        qr = _rope_pairs(q[:, nope_w + LANES * hp: nope_w + LANES * (hp + 1)], cos, sin, first_half)
        qr_hi = pltpu.roll(qr, QK_ROPE_DIM, 1)
        for j, part in ((0, qr), (1, qr_hi)):
            hd = 2 * hp + j
            q_ref[0, hd, :, 0:LANES] = q[:, LANES * hd: LANES * (hd + 1)].astype(BF16)
            q_ref[0, hd, :, LANES:QK_PAD_DIM] = jnp.where(low, part, 0.0).astype(BF16)
    conv_stage(1)

    kvl = _dot(hb, win_ref[:, _C_KVLAT:_C_KR])
    kvn = (_rms(kvl) * kvg_ref[...]).astype(BF16)
    kn = _dot(kvn, wuk_ref[...])
    kr = _rope_pairs(_dot(hb, win_ref[:, _C_KR:_C_END]), cos, sin, first_half)
    kr = jnp.where(low, kr, 0.0).astype(BF16)
    for hd in range(N_HEADS):
        k_ref[0, hd, :, 0:LANES] = kn[:, LANES * hd: LANES * (hd + 1)].astype(BF16)
        k_ref[0, hd, :, LANES:QK_PAD_DIM] = kr
    conv_stage(2)

    vt = lax.dot_general(wuvt_ref[...], kvn, (((1,), (1,)), ((), ())), preferred_element_type=F32)
    for hd in range(N_HEADS):
        vt_ref[0, hd] = vt[V_HEAD_DIM * hd: V_HEAD_DIM * (hd + 1), :].astype(BF16)
    conv_stage(3)

    ya_ref[...] = _dot(act_ref[...], wco_ref[...]).astype(BF16)
    buf_ref[0, 0:CONV_HALO, :] = buf_ref[0, tm:tm + CONV_HALO, :]


def _in_conv(x2, mod3, g_pre, w_in_r, q_norm_g, w_uq_r, kv_norm_g, w_uk, w_uvt, pos_f, inv_freq,
             conv_w8, conv_b8, ln_g, ln_b, w_conv_out, batch, seq, tm, rows):
    nt = seq // tm
    q_scale = (1.0 / math.sqrt(QK_NOPE_DIM + QK_ROPE_DIM)) * math.log2(math.e)
    tok = lambda i: (i, 0)
    heads = lambda i: (i // nt, 0, i % nt, 0)
    return pl.pallas_call(
        functools.partial(_in_conv_kernel, q_scale=q_scale, tm=tm, nt=nt, rows=rows),
        grid=(batch * nt,),
        in_specs=[
            pl.BlockSpec((tm, D_MODEL), tok),
            pl.BlockSpec((1, 6, D_MODEL), lambda i: (i // nt, 0, 0)),
            _const_spec((1, D_MODEL)),
            _const_spec(w_in_r.shape),
            _const_spec((1, Q_LORA_RANK)),
            _const_spec(w_uq_r.shape),
            _const_spec((1, KV_LORA_RANK)),
            _const_spec(w_uk.shape),
            _const_spec(w_uvt.shape),
            pl.BlockSpec((1, 1, tm), lambda i: (i, 0, 0)),
            _const_spec((QK_ROPE_DIM // 2, 1)),
            _const_spec(conv_w8.shape),
            _const_spec((SUBLANES, CONV_DIM)),
            _const_spec((1, CONV_DIM)),
            _const_spec((1, CONV_DIM)),
            _const_spec(w_conv_out.shape),
        ],
        out_specs=[
            pl.BlockSpec((tm, D_MODEL), tok),
            pl.BlockSpec((tm, 2 * D_MODEL), tok),
            pl.BlockSpec((1, N_HEADS, tm, QK_PAD_DIM), heads),
            pl.BlockSpec((1, N_HEADS, tm, QK_PAD_DIM), heads),
            pl.BlockSpec((1, N_HEADS, V_HEAD_DIM, tm), lambda i: (i // nt, 0, 0, i % nt)),
        ],
        out_shape=[
            jax.ShapeDtypeStruct((batch * seq, D_MODEL), BF16),
            jax.ShapeDtypeStruct((batch * seq, 2 * D_MODEL), BF16),
            jax.ShapeDtypeStruct((batch, N_HEADS, seq, QK_PAD_DIM), BF16),
            jax.ShapeDtypeStruct((batch, N_HEADS, seq, QK_PAD_DIM), BF16),
            jax.ShapeDtypeStruct((batch, N_HEADS, V_HEAD_DIM, seq), BF16),
        ],
        scratch_shapes=[pltpu.VMEM((SUBLANES, CONV_HALO + tm, CONV_DIM), F32),
                        pltpu.VMEM((tm, CONV_DIM), BF16)],
        compiler_params=_params("arbitrary"),
        name="in_conv",
    )(x2, mod3, g_pre, w_in_r, q_norm_g, w_uq_r, kv_norm_g, w_uk, w_uvt,
      pos_f.reshape(batch * nt, 1, tm), inv_freq,
      conv_w8, conv_b8, ln_g, ln_b, w_conv_out)


def _attn_kernel(q_ref, k_ref, vt_ref, o_ref, m_ref, l_ref, acc_ref, s_ref, *, tq, tk_full, tk_diag):
    qi = pl.program_id(2)
    n_full = qi * (tq // tk_full)
    n_diag = tq // tk_diag
    diag_start = qi * tq
    m_ref[...] = jnp.full(m_ref.shape, -1e30, F32)
    l_ref[...] = jnp.zeros(l_ref.shape, F32)
    acc_ref[...] = jnp.zeros(acc_ref.shape, F32)

    def scores(hd, start, tk, q0=0):
        k = k_ref[0, hd, pl.ds(start, tk), :]
        s_ref[hd, :tk, q0:] = lax.dot_general(k, q_ref[0, hd, q0:, :], (((1,), (1,)), ((), ())),
                                              preferred_element_type=F32)

    def update(hd, start, tk, q0=0, diag=False):
        s = s_ref[hd, :tk, q0:]
        if diag:
            kpos = lax.broadcasted_iota(jnp.int32, (tk, tk), 0)
            qpos = lax.broadcasted_iota(jnp.int32, (tk, tk), 1)
            tri = jnp.where(qpos >= kpos, s[:, :tk], -1e30)
            s = tri if q0 + tk == tq else jnp.concatenate([tri, s[:, tk:]], axis=1)
        m_old = m_ref[hd, :, q0:]
        m_new = jnp.maximum(m_old, jnp.max(s, axis=0, keepdims=True))
        alpha = jnp.exp2(m_old - m_new)
        p = jnp.exp2(s - m_new)
        m_ref[hd, :, q0:] = m_new
        l_ref[hd, :, q0:] = alpha * l_ref[hd, :, q0:] + jnp.sum(p, axis=0, keepdims=True)
        vt = vt_ref[0, hd, :, pl.ds(start, tk)]
        acc_ref[hd, :, q0:] = alpha * acc_ref[hd, :, q0:] + _dot(vt, p.astype(BF16))

    def full_start(j):
        return pl.multiple_of(j * tk_full, tk_full)

    def diag_tile(d):
        return pl.multiple_of(diag_start + d * tk_diag, tk_diag), tk_diag, d * tk_diag

    if tk_full == tk_diag:
        scores(0, full_start(0), tk_full)
    else:
        pl.when(n_full > 0)(lambda: scores(0, full_start(0), tk_full))
        pl.when(n_full == 0)(lambda: scores(0, *diag_tile(0)))

    def full_pair(j, next_scores):
        scores(1, full_start(j), tk_full)
        update(0, full_start(j), tk_full)
        next_scores()
        update(1, full_start(j), tk_full)

    def body(j, carry):
        full_pair(j, lambda: scores(0, full_start(j + 1), tk_full))
        return carry

    lax.fori_loop(0, jnp.maximum(n_full - 1, 0), body, 0)
    pl.when(n_full > 0)(lambda: full_pair(n_full - 1, lambda: scores(0, *diag_tile(0))))
    for d in range(n_diag):
        scores(1, *diag_tile(d))
        update(0, *diag_tile(d), True)
        if d + 1 < n_diag:
            scores(0, *diag_tile(d + 1))
        update(1, *diag_tile(d), True)
    for hd in range(2):
        o = acc_ref[hd] / l_ref[hd]
        o_ref[0, :, V_HEAD_DIM * hd: V_HEAD_DIM * (hd + 1)] = o.T.astype(BF16)


def _attn(q4, k4, vt4, tq, tk_full, tk_diag):
    batch, heads, seq, _ = q4.shape
    hpb = 2
    return pl.pallas_call(
        functools.partial(_attn_kernel, tq=tq, tk_full=tk_full, tk_diag=tk_diag),
        grid=(batch, heads // hpb, seq // tq),
        in_specs=[
            pl.BlockSpec((1, hpb, tq, QK_PAD_DIM), lambda b, h, i: (b, h, i, 0)),
            pl.BlockSpec((1, hpb, seq, QK_PAD_DIM), lambda b, h, i: (b, h, 0, 0)),
            pl.BlockSpec((1, hpb, V_HEAD_DIM, seq), lambda b, h, i: (b, h, 0, 0)),
        ],
        out_specs=pl.BlockSpec((1, tq, hpb * V_HEAD_DIM), lambda b, h, i: (b, i, h)),
        out_shape=jax.ShapeDtypeStruct((batch, seq, heads * V_HEAD_DIM), BF16),
        scratch_shapes=[pltpu.VMEM((hpb, 1, tq), F32), pltpu.VMEM((hpb, 1, tq), F32),
                        pltpu.VMEM((hpb, V_HEAD_DIM, tq), F32),
                        pltpu.VMEM((hpb, max(tk_full, tk_diag), tq), F32)],
        compiler_params=_params("arbitrary", "arbitrary", "arbitrary"),
        name="attn",
    )(q4, k4, vt4)


def _mix_out_kernel(o_ref, ya_ref, gate_ref, x_ref, mod_ref, gpost_ref, wao_ref, wout_ref, x1_ref):
    y_b = _dot(o_ref[...], wao_ref[...])
    g_a = gate_ref[:, :D_MODEL].astype(F32)
    g_b = gate_ref[:, D_MODEL:].astype(F32)
    merged = g_a * ya_ref[...].astype(F32) + g_b * y_b
    y = _dot(merged.astype(BF16), wout_ref[...])
    gate1 = mod_ref[0, 2:3, :]
    x1_ref[...] = x_ref[...] + gate1 * (_rms(y) * gpost_ref[...])


def _mix_out(o2, ya2, gates, x2, mod3, g_post, w_attn_out, w_out, seq, tm):
    tokens = x2.shape[0]
    nt = seq // tm
    tok = lambda i: (i, 0)
    return pl.pallas_call(
        _mix_out_kernel,
        grid=(tokens // tm,),
        in_specs=[
            pl.BlockSpec((tm, D_MODEL), tok),
            pl.BlockSpec((tm, D_MODEL), tok),
            pl.BlockSpec((tm, 2 * D_MODEL), tok),
            pl.BlockSpec((tm, D_MODEL), tok),
            pl.BlockSpec((1, 6, D_MODEL), lambda i: (i // nt, 0, 0)),
            _const_spec((1, D_MODEL)),
            _const_spec(w_attn_out.shape),
            _const_spec(w_out.shape),
        ],
        out_specs=pl.BlockSpec((tm, D_MODEL), tok),
        out_shape=jax.ShapeDtypeStruct((tokens, D_MODEL), F32),
        compiler_params=_params("arbitrary"),
        name="mix_out",
    )(o2, ya2, gates, x2, mod3, g_post, w_attn_out, w_out)


def _mlp_kernel(x_ref, mod_ref, gpre_ref, gpost_ref, w1_ref, w2_ref, o_ref):
    x = x_ref[...]
    shift = mod_ref[0, 3:4, :]
    scale = mod_ref[0, 4:5, :]
    gate = mod_ref[0, 5:6, :]
    h = (_rms(x) * gpre_ref[...]) * (1.0 + scale) + shift
    hid = jnp.maximum(_dot(h.astype(BF16), w1_ref[...]), 0.0)
    y = _dot((hid * hid).astype(BF16), w2_ref[...])
    o_ref[...] = x + gate * (_rms(y) * gpost_ref[...])


def _mlp(x1, mod3, g_pre, g_post, w1, w2, seq, tm):
    tokens = x1.shape[0]
    nt = seq // tm
    tok = lambda i: (i, 0)
    return pl.pallas_call(
        _mlp_kernel,
        grid=(tokens // tm,),
        in_specs=[
            pl.BlockSpec((tm, D_MODEL), tok),
            pl.BlockSpec((1, 6, D_MODEL), lambda i: (i // nt, 0, 0)),
            _const_spec((1, D_MODEL)),
            _const_spec((1, D_MODEL)),
            _const_spec(w1.shape),
            _const_spec(w2.shape),
        ],
        out_specs=pl.BlockSpec((tm, D_MODEL), tok),
        out_shape=jax.ShapeDtypeStruct((tokens, D_MODEL), F32),
        compiler_params=_params("arbitrary"),
        name="mlp",
    )(x1, mod3, g_pre, g_post, w1, w2)


def _head_major_cols(w, split):
    k = w.shape[0]
    w3 = w.reshape(k, N_HEADS, -1)
    return jnp.concatenate([w3[:, :, :split].reshape(k, -1), w3[:, :, split:].reshape(k, -1)], axis=1)


def kernel(x, c, positions, w_ada, b_ada, g_pre_mix, g_post_mix, g_pre_mlp, g_post_mlp, w_in, conv_w, conv_b, conv_norm_g, conv_norm_b, w_conv_out, q_norm_g, w_uq, kv_norm_g, w_ukv, w_attn_out, w_out, w_mlp_in, w_mlp_out):
    batch, seq, d = x.shape
    depth = w_ada.shape[0]
    tokens = batch * seq
    row = lambda v: v.reshape(1, -1)

    inv_freq = 1.0 / (ROPE_THETA ** (jnp.arange(0, QK_ROPE_DIM, 2, dtype=F32) / QK_ROPE_DIM))
    inv_freq = inv_freq.reshape(-1, 1)
    pos_f = positions.astype(F32)
    c_pad = jnp.pad(c, ((0, 8 - batch), (0, 0)))

    x2 = x.reshape(tokens, d)
    for l in range(depth):
        wi = w_in[l].astype(BF16)
        s0, s1, s2, s3 = 2 * CONV_DIM, 2 * CONV_DIM + Q_LORA_RANK, 2 * CONV_DIM + Q_LORA_RANK + KV_LORA_RANK, \
            2 * CONV_DIM + Q_LORA_RANK + KV_LORA_RANK + QK_ROPE_DIM
        w_in_r = jnp.concatenate([wi[:, :s0], wi[:, s3:], wi[:, s0:s1], wi[:, s1:s2], wi[:, s2:s3], wi[:, s2:s3]],
                                 axis=1)
        w_uq_r = _head_major_cols(w_uq[l], QK_NOPE_DIM).astype(BF16)
        w_ukv_r = _head_major_cols(w_ukv[l], QK_NOPE_DIM).astype(BF16)
        w_uk = w_ukv_r[:, :N_HEADS * QK_NOPE_DIM]
        w_uvt = w_ukv_r[:, N_HEADS * QK_NOPE_DIM:].T

        mod = _ada(c_pad, w_ada[l], row(b_ada[l]))
        mod3 = mod[:batch].reshape(batch, 6, d)

        conv_w8 = jnp.broadcast_to(conv_w[l][:, None, :], (CONV_KERNEL, SUBLANES, CONV_DIM))
        conv_b8 = jnp.broadcast_to(row(conv_b[l]), (SUBLANES, CONV_DIM))
        y_a, gates, q4, k4, vt4 = _in_conv(x2, mod3, row(g_pre_mix[l]), w_in_r, row(q_norm_g[l]), w_uq_r,
                                           row(kv_norm_g[l]), w_uk, w_uvt, pos_f, inv_freq, conv_w8, conv_b8,
                                           row(conv_norm_g[l]), row(conv_norm_b[l]), w_conv_out[l].astype(BF16),
                                           batch, seq, tm=256, rows=32)
        o = _attn(q4, k4, vt4, tq=1024, tk_full=512, tk_diag=512)
        x2 = _mix_out(o.reshape(tokens, d), y_a, gates, x2, mod3, row(g_post_mix[l]),
                      w_attn_out[l].astype(BF16), w_out[l].astype(BF16), seq, tm=512)
        x2 = _mlp(x2, mod3, row(g_pre_mlp[l]), row(g_post_mlp[l]), w_mlp_in[l].astype(BF16),
                  w_mlp_out[l].astype(BF16), seq, tm=512)
    return x2.reshape(batch, seq, d)
```

```python
import functools
import math

import jax
import jax.numpy as jnp
from jax import lax
from jax.experimental import pallas as pl
from jax.experimental.pallas import tpu as pltpu

D_MODEL = 1024
CONV_DIM = 1024
CONV_KERNEL = 31
N_HEADS = 8
QK_NOPE_DIM = 128
QK_ROPE_DIM = 64
V_HEAD_DIM = 128
Q_LORA_RANK = 384
KV_LORA_RANK = 256
ROPE_THETA = 10000.0
D_FF = 4 * D_MODEL
EPS = 1e-6

LANES = 128
SUBLANES = 8
QK_PAD_DIM = 2 * LANES
CONV_HALO = 32
VMEM_LIMIT = 56 * 1024 * 1024

_C_GLU = 0
_C_GATE = 2 * CONV_DIM
_C_QLAT = _C_GATE + 2 * D_MODEL
_C_KVLAT = _C_QLAT + Q_LORA_RANK
_C_KR = _C_KVLAT + KV_LORA_RANK
_C_END = _C_KR + LANES

BF16 = jnp.bfloat16
F32 = jnp.float32


def _dot(a, b):
    return jnp.dot(a, b, preferred_element_type=F32)


def _rms(v):
    return v * lax.rsqrt(jnp.mean(v * v, axis=-1, keepdims=True) + EPS)


def _params(*sem):
    return pltpu.CompilerParams(dimension_semantics=sem, vmem_limit_bytes=VMEM_LIMIT)


def _const_spec(shape):
    nd = len(shape)
    return pl.BlockSpec(shape, lambda *_: (0,) * nd, pipeline_mode=pl.Buffered(1))


def _ada_kernel(c_ref, w_ref, b_ref, o_ref):
    c = c_ref[...]
    act = c * jax.nn.sigmoid(c)
    o_ref[...] = _dot(act.astype(BF16), w_ref[...].astype(BF16)) + b_ref[...]


def _ada(c_pad, w_ada, b_ada):
    rows, d = c_pad.shape
    n = w_ada.shape[1]
    tn = 1024
    return pl.pallas_call(
        _ada_kernel,
        grid=(n // tn,),
        in_specs=[
            pl.BlockSpec((rows, d), lambda j: (0, 0)),
            pl.BlockSpec((d, tn), lambda j: (0, j)),
            pl.BlockSpec((1, tn), lambda j: (0, j)),
        ],
        out_specs=pl.BlockSpec((rows, tn), lambda j: (0, j)),
        out_shape=jax.ShapeDtypeStruct((rows, n), F32),
        compiler_params=_params("arbitrary"),
        name="ada",
    )(c_pad, w_ada, b_ada)


_S_QLAT = 2 * CONV_DIM
_S_KVLAT = _S_QLAT + Q_LORA_RANK
_S_KR = _S_KVLAT + KV_LORA_RANK
_S_GATE = _S_KR + QK_ROPE_DIM


def _prep_w_in_kernel(w_ref, o_ref):
    rows = w_ref.shape[0]
    o_ref[:, _C_GLU:_C_GATE] = w_ref[:, 0:_S_QLAT].astype(BF16)
    o_ref[:, _C_QLAT:_C_KVLAT] = w_ref[:, _S_QLAT:_S_KVLAT].astype(BF16)
    o_ref[:, _C_KVLAT:_C_KR] = w_ref[:, _S_KVLAT:_S_KR].astype(BF16)
    lane = lax.broadcasted_iota(jnp.int32, (rows, LANES), 1)
    low = lane < QK_ROPE_DIM
    n_tiles = (_C_QLAT - _C_GATE) // LANES
    for i in range(n_tiles):
        a = w_ref[:, _S_KR + LANES * i: _S_KR + LANES * (i + 1)]
        if i + 1 < n_tiles:
            b = w_ref[:, _S_KR + LANES * (i + 1): _S_KR + LANES * (i + 2)]
        else:
            last = w_ref[:, _S_KR + LANES * n_tiles:]
            b = jnp.concatenate([last, jnp.zeros_like(last)], axis=1)
        tile = jnp.where(low, pltpu.roll(a, QK_ROPE_DIM, 1), pltpu.roll(b, QK_ROPE_DIM, 1))
        o_ref[:, _C_GATE + LANES * i: _C_GATE + LANES * (i + 1)] = tile.astype(BF16)
    kr = w_ref[:, _S_KR:_S_KR + LANES]
    o_ref[:, _C_KR:_C_END] = jnp.where(low, kr, pltpu.roll(kr, QK_ROPE_DIM, 1)).astype(BF16)


def _prep_w_in(w):
    k, n = w.shape
    tr = 128
    return pl.pallas_call(
        _prep_w_in_kernel,
        grid=(k // tr,),
        in_specs=[pl.BlockSpec((tr, n), lambda i: (i, 0))],
        out_specs=pl.BlockSpec((tr, _C_END), lambda i: (i, 0)),
        out_shape=jax.ShapeDtypeStruct((k, _C_END), BF16),
        compiler_params=_params("arbitrary"),
        name="prep_w_in",
    )(w)


def _rope_pairs(t, cos, sin, first_half):
    rot = jnp.where(first_half, -pltpu.roll(t, LANES - QK_ROPE_DIM // 2, 1),
                    pltpu.roll(t, QK_ROPE_DIM // 2, 1))
    return t * cos + rot * sin


def _in_conv_kernel(x_ref, mod_ref, gpre_ref, win_ref, qg_ref, wuq_ref, kvg_ref, wuk_ref, wuvt_ref,
                    pos_ref, invf_ref, cw_ref, cb_ref, lng_ref, lnb_ref, wco_ref,
                    ya_ref, gate_ref, q_ref, k_ref, vt_ref, buf_ref, act_ref, *, q_scale, tm, nt, rows):
    t = pl.program_id(0) % nt

    @pl.when(t == 0)
    def _():
        buf_ref[0, 0:CONV_HALO, :] = jnp.zeros((CONV_HALO, CONV_DIM), F32)

    x = x_ref[...]
    shift = mod_ref[0, 0:1, :]
    scale = mod_ref[0, 1:2, :]
    h = (_rms(x) * gpre_ref[...]) * (1.0 + scale) + shift
    hb = h.astype(BF16)

    zg = _dot(hb, win_ref[:, _C_GLU:_C_GATE])
    buf_ref[0, CONV_HALO:, :] = zg[:, :CONV_DIM] * jax.nn.sigmoid(zg[:, CONV_DIM:])
    n_shift = tm + CONV_HALO - SUBLANES
    for b in range(1, SUBLANES):
        buf_ref[b, 0:n_shift, :] = buf_ref[0, b:b + n_shift, :]
    first = CONV_HALO - (CONV_KERNEL - 1)
    groups = rows // SUBLANES
    half = CONV_DIM // 2

    def conv_chunk(r0):
        halves = []
        for c0 in (0, half):
            acc = jnp.broadcast_to(cb_ref[:, c0:c0 + half][None], (groups, SUBLANES, half))
            for k in range(CONV_KERNEL):
                a, b = divmod(first + k, SUBLANES)
                win = buf_ref[b, r0 + SUBLANES * a: r0 + SUBLANES * a + rows, c0:c0 + half]
                acc = acc + cw_ref[k, :, c0:c0 + half][None] * win.reshape(groups, SUBLANES, half)
            halves.append(acc.reshape(rows, half))
        acc = jnp.concatenate(halves, axis=-1)
        mu = jnp.mean(acc, axis=-1, keepdims=True)
        cen = acc - mu
        var = jnp.mean(cen * cen, axis=-1, keepdims=True)
        y = cen * lax.rsqrt(var + EPS) * lng_ref[...] + lnb_ref[...]
        y = y * jax.nn.sigmoid(y)
        act_ref[r0:r0 + rows, :] = y.astype(BF16)

    chunk_starts = list(range(0, tm, rows))
    per_stage = -(-len(chunk_starts) // 4)

    def conv_stage(i):
        for r0 in chunk_starts[i * per_stage:(i + 1) * per_stage]:
            conv_chunk(r0)

    gate_ref[...] = jax.nn.sigmoid(_dot(hb, win_ref[:, _C_GATE:_C_QLAT])).astype(BF16)
    conv_stage(0)

    ang_t = invf_ref[...] * pos_ref[0]
    reps = LANES // (QK_ROPE_DIM // 2)
    cos = jnp.concatenate([jnp.cos(ang_t)] * reps, axis=0).T
    sin = jnp.concatenate([jnp.sin(ang_t)] * reps, axis=0).T
    lane = lax.broadcasted_iota(jnp.int32, cos.shape, 1)
    first_half = (lane % QK_ROPE_DIM) < (QK_ROPE_DIM // 2)
    low = lane < QK_ROPE_DIM

    ql = _dot(hb, win_ref[:, _C_QLAT:_C_KVLAT])
    qn = (_rms(ql) * qg_ref[...]).astype(BF16)
    q = _dot(qn, wuq_ref[...]) * q_scale
    nope_w = N_HEADS * QK_NOPE_DIM
    for hp in range(N_HEADS // 2):
        qr = _rope_pairs(q[:, nope_w + LANES * hp: nope_w + LANES * (hp + 1)], cos, sin, first_half)
        qr_hi = pltpu.roll(qr, QK_ROPE_DIM, 1)
        for j, part in ((0, qr), (1, qr_hi)):
            hd = 2 * hp + j
            q_ref[0, hd, :, 0:LANES] = q[:, LANES * hd: LANES * (hd + 1)].astype(BF16)
            q_ref[0, hd, :, LANES:QK_PAD_DIM] = jnp.where(low, part, 0.0).astype(BF16)
    conv_stage(1)

    kvl = _dot(hb, win_ref[:, _C_KVLAT:_C_KR])
    kvn = (_rms(kvl) * kvg_ref[...]).astype(BF16)
    kn = _dot(kvn, wuk_ref[...])
    kr = _rope_pairs(_dot(hb, win_ref[:, _C_KR:_C_END]), cos, sin, first_half)
    kr = jnp.where(low, kr, 0.0).astype(BF16)
    for hd in range(N_HEADS):
        k_ref[0, hd, :, 0:LANES] = kn[:, LANES * hd: LANES * (hd + 1)].astype(BF16)
        k_ref[0, hd, :, LANES:QK_PAD_DIM] = kr
    conv_stage(2)

    vt = lax.dot_general(wuvt_ref[...], kvn, (((1,), (1,)), ((), ())), preferred_element_type=F32)
    for hd in range(N_HEADS):
        vt_ref[0, hd] = vt[V_HEAD_DIM * hd: V_HEAD_DIM * (hd + 1), :].astype(BF16)
    conv_stage(3)

    ya_ref[...] = _dot(act_ref[...], wco_ref[...]).astype(BF16)
    buf_ref[0, 0:CONV_HALO, :] = buf_ref[0, tm:tm + CONV_HALO, :]


def _in_conv(x2, mod3, g_pre, w_in_r, q_norm_g, w_uq_r, kv_norm_g, w_uk, w_uvt, pos_f, inv_freq,
             conv_w8, conv_b8, ln_g, ln_b, w_conv_out, batch, seq, tm, rows):
    nt = seq // tm
    q_scale = (1.0 / math.sqrt(QK_NOPE_DIM + QK_ROPE_DIM)) * math.log2(math.e)
    tok = lambda i: (i, 0)
    heads = lambda i: (i // nt, 0, i % nt, 0)
    return pl.pallas_call(
        functools.partial(_in_conv_kernel, q_scale=q_scale, tm=tm, nt=nt, rows=rows),
        grid=(batch * nt,),
        in_specs=[
            pl.BlockSpec((tm, D_MODEL), tok),
            pl.BlockSpec((1, 6, D_MODEL), lambda i: (i // nt, 0, 0)),
            _const_spec((1, D_MODEL)),
            _const_spec(w_in_r.shape),
            _const_spec((1, Q_LORA_RANK)),
            _const_spec(w_uq_r.shape),
            _const_spec((1, KV_LORA_RANK)),
            _const_spec(w_uk.shape),
            _const_spec(w_uvt.shape),
            pl.BlockSpec((1, 1, tm), lambda i: (i, 0, 0)),
            _const_spec((QK_ROPE_DIM // 2, 1)),
            _const_spec(conv_w8.shape),
            _const_spec((SUBLANES, CONV_DIM)),
            _const_spec((1, CONV_DIM)),
            _const_spec((1, CONV_DIM)),
            _const_spec(w_conv_out.shape),
        ],
        out_specs=[
            pl.BlockSpec((tm, D_MODEL), tok),
            pl.BlockSpec((tm, 2 * D_MODEL), tok),
            pl.BlockSpec((1, N_HEADS, tm, QK_PAD_DIM), heads),
            pl.BlockSpec((1, N_HEADS, tm, QK_PAD_DIM), heads),
            pl.BlockSpec((1, N_HEADS, V_HEAD_DIM, tm), lambda i: (i // nt, 0, 0, i % nt)),
        ],
        out_shape=[
            jax.ShapeDtypeStruct((batch * seq, D_MODEL), BF16),
            jax.ShapeDtypeStruct((batch * seq, 2 * D_MODEL), BF16),
            jax.ShapeDtypeStruct((batch, N_HEADS, seq, QK_PAD_DIM), BF16),
            jax.ShapeDtypeStruct((batch, N_HEADS, seq, QK_PAD_DIM), BF16),
            jax.ShapeDtypeStruct((batch, N_HEADS, V_HEAD_DIM, seq), BF16),
        ],
        scratch_shapes=[pltpu.VMEM((SUBLANES, CONV_HALO + tm, CONV_DIM), F32),
                        pltpu.VMEM((tm, CONV_DIM), BF16)],
        compiler_params=_params("arbitrary"),
        name="in_conv",
    )(x2, mod3, g_pre, w_in_r, q_norm_g, w_uq_r, kv_norm_g, w_uk, w_uvt,
      pos_f.reshape(batch * nt, 1, tm), inv_freq,
      conv_w8, conv_b8, ln_g, ln_b, w_conv_out)


def _attn_kernel(q_ref, k_ref, vt_ref, o_ref, m_ref, l_ref, acc_ref, s_ref, *, tq, tk_full, tk_diag, hpb):
    qi = pl.program_id(2)
    n_full = qi * (tq // tk_full)
    n_diag = tq // tk_diag
    diag_start = qi * tq
    m_ref[...] = jnp.full(m_ref.shape, -1e30, F32)
    l_ref[...] = jnp.zeros(l_ref.shape, F32)
    acc_ref[...] = jnp.zeros(acc_ref.shape, F32)

    def scores(hd, start, tk, q0=0):
        k = k_ref[0, hd, pl.ds(start, tk), :]
        s_ref[hd, :tk, q0:] = lax.dot_general(k, q_ref[0, hd, q0:, :], (((1,), (1,)), ((), ())),
                                              preferred_element_type=F32)

    def update(hd, start, tk, q0=0, diag=False):
        s = s_ref[hd, :tk, q0:]
        if diag:
            kpos = lax.broadcasted_iota(jnp.int32, (tk, tk), 0)
            qpos = lax.broadcasted_iota(jnp.int32, (tk, tk), 1)
            tri = jnp.where(qpos >= kpos, s[:, :tk], -1e30)
            s = tri if q0 + tk == tq else jnp.concatenate([tri, s[:, tk:]], axis=1)
        m_old = m_ref[hd, :, q0:]
        m_new = jnp.maximum(m_old, jnp.max(s, axis=0, keepdims=True))
        alpha = jnp.exp2(m_old - m_new)
        p = jnp.exp2(s - m_new)
        m_ref[hd, :, q0:] = m_new
        l_ref[hd, :, q0:] = alpha * l_ref[hd, :, q0:] + jnp.sum(p, axis=0, keepdims=True)
        vt = vt_ref[0, hd, :, pl.ds(start, tk)]
        acc_ref[hd, :, q0:] = alpha * acc_ref[hd, :, q0:] + _dot(vt, p.astype(BF16))

    def full_start(j):
        return pl.multiple_of(j * tk_full, tk_full)

    def diag_tile(d):
        return pl.multiple_of(diag_start + d * tk_diag, tk_diag), tk_diag, d * tk_diag

    if tk_full == tk_diag:
        scores(0, full_start(0), tk_full)
    else:
        pl.when(n_full > 0)(lambda: scores(0, full_start(0), tk_full))
        pl.when(n_full == 0)(lambda: scores(0, *diag_tile(0)))

    def full_round(j, next_scores):
        for hd in range(hpb):
            if hd + 1 < hpb:
                scores(hd + 1, full_start(j), tk_full)
            else:
                next_scores()
            update(hd, full_start(j), tk_full)

    def body(j, carry):
        full_round(j, lambda: scores(0, full_start(j + 1), tk_full))
        return carry

    lax.fori_loop(0, jnp.maximum(n_full - 1, 0), body, 0)
    pl.when(n_full > 0)(lambda: full_round(n_full - 1, lambda: scores(0, *diag_tile(0))))
    for d in range(n_diag):
        for hd in range(hpb):
            if hd + 1 < hpb:
                scores(hd + 1, *diag_tile(d))
            elif d + 1 < n_diag:
                scores(0, *diag_tile(d + 1))
            update(hd, *diag_tile(d), True)
    for hd in range(hpb):
        o = acc_ref[hd] / l_ref[hd]
        o_ref[0, :, V_HEAD_DIM * hd: V_HEAD_DIM * (hd + 1)] = o.T.astype(BF16)


def _attn(q4, k4, vt4, tq, tk_full, tk_diag, hpb):
    batch, heads, seq, _ = q4.shape
    return pl.pallas_call(
        functools.partial(_attn_kernel, tq=tq, tk_full=tk_full, tk_diag=tk_diag, hpb=hpb),
        grid=(batch, heads // hpb, seq // tq),
        in_specs=[
            pl.BlockSpec((1, hpb, tq, QK_PAD_DIM), lambda b, h, i: (b, h, i, 0)),
            pl.BlockSpec((1, hpb, seq, QK_PAD_DIM), lambda b, h, i: (b, h, 0, 0)),
            pl.BlockSpec((1, hpb, V_HEAD_DIM, seq), lambda b, h, i: (b, h, 0, 0)),
        ],
        out_specs=pl.BlockSpec((1, tq, hpb * V_HEAD_DIM), lambda b, h, i: (b, i, h)),
        out_shape=jax.ShapeDtypeStruct((batch, seq, heads * V_HEAD_DIM), BF16),
        scratch_shapes=[pltpu.VMEM((hpb, 1, tq), F32), pltpu.VMEM((hpb, 1, tq), F32),
                        pltpu.VMEM((hpb, V_HEAD_DIM, tq), F32),
                        pltpu.VMEM((hpb, max(tk_full, tk_diag), tq), F32)],
        compiler_params=_params("arbitrary", "arbitrary", "arbitrary"),
        name="attn",
    )(q4, k4, vt4)


def _mix_out_kernel(o_ref, ya_ref, gate_ref, x_ref, mod_ref, gpost_ref, wao_ref, wout_ref, x1_ref):
    y_b = _dot(o_ref[...], wao_ref[...])
    g_a = gate_ref[:, :D_MODEL].astype(F32)
    g_b = gate_ref[:, D_MODEL:].astype(F32)
    merged = g_a * ya_ref[...].astype(F32) + g_b * y_b
    y = _dot(merged.astype(BF16), wout_ref[...])
    gate1 = mod_ref[0, 2:3, :]
    x1_ref[...] = x_ref[...] + gate1 * (_rms(y) * gpost_ref[...])


def _mix_out(o2, ya2, gates, x2, mod3, g_post, w_attn_out, w_out, seq, tm):
    tokens = x2.shape[0]
    nt = seq // tm
    tok = lambda i: (i, 0)
    return pl.pallas_call(
        _mix_out_kernel,
        grid=(tokens // tm,),
        in_specs=[
            pl.BlockSpec((tm, D_MODEL), tok),
            pl.BlockSpec((tm, D_MODEL), tok),
            pl.BlockSpec((tm, 2 * D_MODEL), tok),
            pl.BlockSpec((tm, D_MODEL), tok),
            pl.BlockSpec((1, 6, D_MODEL), lambda i: (i // nt, 0, 0)),
            _const_spec((1, D_MODEL)),
            _const_spec(w_attn_out.shape),
            _const_spec(w_out.shape),
        ],
        out_specs=pl.BlockSpec((tm, D_MODEL), tok),
        out_shape=jax.ShapeDtypeStruct((tokens, D_MODEL), F32),
        compiler_params=_params("arbitrary"),
        name="mix_out",
    )(o2, ya2, gates, x2, mod3, g_post, w_attn_out, w_out)


def _mlp_kernel(x_ref, mod_ref, gpre_ref, gpost_ref, w1_ref, w2_ref, o_ref):
    x = x_ref[...]
    shift = mod_ref[0, 3:4, :]
    scale = mod_ref[0, 4:5, :]
    gate = mod_ref[0, 5:6, :]
    h = (_rms(x) * gpre_ref[...]) * (1.0 + scale) + shift
    hid = jnp.maximum(_dot(h.astype(BF16), w1_ref[...]), 0.0)
    y = _dot((hid * hid).astype(BF16), w2_ref[...])
    o_ref[...] = x + gate * (_rms(y) * gpost_ref[...])


def _mlp(x1, mod3, g_pre, g_post, w1, w2, seq, tm):
    tokens = x1.shape[0]
    nt = seq // tm
    tok = lambda i: (i, 0)
    return pl.pallas_call(
        _mlp_kernel,
        grid=(tokens // tm,),
        in_specs=[
            pl.BlockSpec((tm, D_MODEL), tok),
            pl.BlockSpec((1, 6, D_MODEL), lambda i: (i // nt, 0, 0)),
            _const_spec((1, D_MODEL)),
            _const_spec((1, D_MODEL)),
            _const_spec(w1.shape),
            _const_spec(w2.shape),
        ],
        out_specs=pl.BlockSpec((tm, D_MODEL), tok),
        out_shape=jax.ShapeDtypeStruct((tokens, D_MODEL), F32),
        compiler_params=_params("arbitrary"),
        name="mlp",
    )(x1, mod3, g_pre, g_post, w1, w2)


def _head_major_cols(w, split):
    k = w.shape[0]
    w3 = w.reshape(k, N_HEADS, -1)
    return jnp.concatenate([w3[:, :, :split].reshape(k, -1), w3[:, :, split:].reshape(k, -1)], axis=1)


def kernel(x, c, positions, w_ada, b_ada, g_pre_mix, g_post_mix, g_pre_mlp, g_post_mlp, w_in, conv_w, conv_b, conv_norm_g, conv_norm_b, w_conv_out, q_norm_g, w_uq, kv_norm_g, w_ukv, w_attn_out, w_out, w_mlp_in, w_mlp_out):
    batch, seq, d = x.shape
    depth = w_ada.shape[0]
    tokens = batch * seq
    row = lambda v: v.reshape(1, -1)

    inv_freq = 1.0 / (ROPE_THETA ** (jnp.arange(0, QK_ROPE_DIM, 2, dtype=F32) / QK_ROPE_DIM))
    inv_freq = inv_freq.reshape(-1, 1)
    pos_f = positions.astype(F32)
    c_pad = jnp.pad(c, ((0, 8 - batch), (0, 0)))

    x2 = x.reshape(tokens, d)
    for l in range(depth):
        w_in_r = _prep_w_in(w_in[l])
        w_uq_r = _head_major_cols(w_uq[l], QK_NOPE_DIM).astype(BF16)
        w_ukv_r = _head_major_cols(w_ukv[l], QK_NOPE_DIM).astype(BF16)
        w_uk = w_ukv_r[:, :N_HEADS * QK_NOPE_DIM]
        w_uvt = w_ukv_r[:, N_HEADS * QK_NOPE_DIM:].T

        mod = _ada(c_pad, w_ada[l], row(b_ada[l]))
        mod3 = mod[:batch].reshape(batch, 6, d)

        conv_w8 = jnp.broadcast_to(conv_w[l][:, None, :], (CONV_KERNEL, SUBLANES, CONV_DIM))
        conv_b8 = jnp.broadcast_to(row(conv_b[l]), (SUBLANES, CONV_DIM))
        y_a, gates, q4, k4, vt4 = _in_conv(x2, mod3, row(g_pre_mix[l]), w_in_r, row(q_norm_g[l]), w_uq_r,
                                           row(kv_norm_g[l]), w_uk, w_uvt, pos_f, inv_freq, conv_w8, conv_b8,
                                           row(conv_norm_g[l]), row(conv_norm_b[l]), w_conv_out[l].astype(BF16),
                                           batch, seq, tm=256, rows=32)
        o = _attn(q4, k4, vt4, tq=1024, tk_full=512, tk_diag=512, hpb=4)
        x2 = _mix_out(o.reshape(tokens, d), y_a, gates, x2, mod3, row(g_post_mix[l]),
                      w_attn_out[l].astype(BF16), w_out[l].astype(BF16), seq, tm=512)
        x2 = _mlp(x2, mod3, row(g_pre_mlp[l]), row(g_post_mlp[l]), w_mlp_in[l].astype(BF16),
                  w_mlp_out[l].astype(BF16), seq, tm=512)
    return x2.reshape(batch, seq, d)
```

```python
import functools
import math

import jax
import jax.numpy as jnp
from jax import lax
from jax.experimental import pallas as pl
from jax.experimental.pallas import tpu as pltpu

D_MODEL = 1024
CONV_DIM = 1024
CONV_KERNEL = 31
N_HEADS = 8
QK_NOPE_DIM = 128
QK_ROPE_DIM = 64
V_HEAD_DIM = 128
Q_LORA_RANK = 384
KV_LORA_RANK = 256
ROPE_THETA = 10000.0
D_FF = 4 * D_MODEL
EPS = 1e-6

LANES = 128
SUBLANES = 8
QK_PAD_DIM = 2 * LANES
CONV_HALO = 32
VMEM_LIMIT = 56 * 1024 * 1024

_C_GLU = 0
_C_GATE = 2 * CONV_DIM
_C_QLAT = _C_GATE + 2 * D_MODEL
_C_KVLAT = _C_QLAT + Q_LORA_RANK
_C_KR = _C_KVLAT + KV_LORA_RANK
_C_END = _C_KR + LANES

BF16 = jnp.bfloat16
F32 = jnp.float32


def _dot(a, b):
    return jnp.dot(a, b, preferred_element_type=F32)


def _rms(v):
    return v * lax.rsqrt(jnp.mean(v * v, axis=-1, keepdims=True) + EPS)


def _params(*sem):
    return pltpu.CompilerParams(dimension_semantics=sem, vmem_limit_bytes=VMEM_LIMIT)


def _const_spec(shape):
    nd = len(shape)
    return pl.BlockSpec(shape, lambda *_: (0,) * nd, pipeline_mode=pl.Buffered(1))


def _ada_kernel(c_ref, w_ref, b_ref, o_ref):
    c = c_ref[...]
    act = c * jax.nn.sigmoid(c)
    o_ref[...] = _dot(act.astype(BF16), w_ref[...].astype(BF16)) + b_ref[...]


def _ada(c_pad, w_ada, b_ada):
    rows, d = c_pad.shape
    n = w_ada.shape[1]
    tn = 1024
    return pl.pallas_call(
        _ada_kernel,
        grid=(n // tn,),
        in_specs=[
            pl.BlockSpec((rows, d), lambda j: (0, 0)),
            pl.BlockSpec((d, tn), lambda j: (0, j)),
            pl.BlockSpec((1, tn), lambda j: (0, j)),
        ],
        out_specs=pl.BlockSpec((rows, tn), lambda j: (0, j)),
        out_shape=jax.ShapeDtypeStruct((rows, n), F32),
        compiler_params=_params("arbitrary"),
        name="ada",
    )(c_pad, w_ada, b_ada)


_S_QLAT = 2 * CONV_DIM
_S_KVLAT = _S_QLAT + Q_LORA_RANK
_S_KR = _S_KVLAT + KV_LORA_RANK
_S_GATE = _S_KR + QK_ROPE_DIM


def _prep_w_in_kernel(wt_ref, o_ref):
    def move(dst, src, n):
        for off in range(0, n, QK_PAD_DIM):
            m = min(QK_PAD_DIM, n - off)
            o_ref[:, dst + off:dst + off + m] = wt_ref[src + off:src + off + m, :].T.astype(BF16)

    move(_C_GLU, 0, _S_QLAT)
    move(_C_GATE, _S_GATE, _C_QLAT - _C_GATE)
    move(_C_QLAT, _S_QLAT, Q_LORA_RANK)
    move(_C_KVLAT, _S_KVLAT, KV_LORA_RANK)
    kr = wt_ref[_S_KR:_S_KR + LANES, :].T
    lane = lax.broadcasted_iota(jnp.int32, kr.shape, 1)
    o_ref[:, _C_KR:_C_END] = jnp.where(lane < QK_ROPE_DIM, kr, pltpu.roll(kr, QK_ROPE_DIM, 1)).astype(BF16)


def _prep_w_in(w):
    k, n = w.shape
    return pl.pallas_call(
        _prep_w_in_kernel,
        grid=(1,),
        in_specs=[_const_spec((n, k))],
        out_specs=pl.BlockSpec((k, _C_END), lambda i: (0, 0)),
        out_shape=jax.ShapeDtypeStruct((k, _C_END), BF16),
        compiler_params=_params("arbitrary"),
        name="prep_w_in",
    )(w.T)


def _rope_pairs(t, cos, sin, first_half):
    rot = jnp.where(first_half, -pltpu.roll(t, LANES - QK_ROPE_DIM // 2, 1),
                    pltpu.roll(t, QK_ROPE_DIM // 2, 1))
    return t * cos + rot * sin


def _in_conv_kernel(x_ref, mod_ref, gpre_ref, win_ref, qg_ref, wuq_ref, kvg_ref, wuk_ref, wuvt_ref,
                    pos_ref, invf_ref, cw_ref, cb_ref, lng_ref, lnb_ref, wco_ref,
                    ya_ref, gate_ref, q_ref, k_ref, vt_ref, buf_ref, act_ref, *, q_scale, tm, nt, rows):
    t = pl.program_id(0) % nt

    @pl.when(t == 0)
    def _():
        buf_ref[0, 0:CONV_HALO, :] = jnp.zeros((CONV_HALO, CONV_DIM), F32)

    x = x_ref[...]
    shift = mod_ref[0, 0:1, :]
    scale = mod_ref[0, 1:2, :]
    h = (_rms(x) * gpre_ref[...]) * (1.0 + scale) + shift
    hb = h.astype(BF16)

    zg = _dot(hb, win_ref[:, _C_GLU:_C_GATE])
    buf_ref[0, CONV_HALO:, :] = zg[:, :CONV_DIM] * jax.nn.sigmoid(zg[:, CONV_DIM:])
    n_shift = tm + CONV_HALO - SUBLANES
    for b in range(1, SUBLANES):
        buf_ref[b, 0:n_shift, :] = buf_ref[0, b:b + n_shift, :]
    first = CONV_HALO - (CONV_KERNEL - 1)
    groups = rows // SUBLANES
    half = CONV_DIM // 2

    def conv_chunk(r0):
        halves = []
        for c0 in (0, half):
            acc = jnp.broadcast_to(cb_ref[:, c0:c0 + half][None], (groups, SUBLANES, half))
            for k in range(CONV_KERNEL):
                a, b = divmod(first + k, SUBLANES)
                win = buf_ref[b, r0 + SUBLANES * a: r0 + SUBLANES * a + rows, c0:c0 + half]
                acc = acc + cw_ref[k, :, c0:c0 + half][None] * win.reshape(groups, SUBLANES, half)
            halves.append(acc.reshape(rows, half))
        acc = jnp.concatenate(halves, axis=-1)
        mu = jnp.mean(acc, axis=-1, keepdims=True)
        cen = acc - mu
        var = jnp.mean(cen * cen, axis=-1, keepdims=True)
        y = cen * lax.rsqrt(var + EPS) * lng_ref[...] + lnb_ref[...]
        y = y * jax.nn.sigmoid(y)
        act_ref[r0:r0 + rows, :] = y.astype(BF16)

    chunk_starts = list(range(0, tm, rows))
    per_stage = -(-len(chunk_starts) // 4)

    def conv_stage(i):
        for r0 in chunk_starts[i * per_stage:(i + 1) * per_stage]:
            conv_chunk(r0)

    gate_ref[...] = jax.nn.sigmoid(_dot(hb, win_ref[:, _C_GATE:_C_QLAT])).astype(BF16)
    conv_stage(0)

    ang_t = invf_ref[...] * pos_ref[0]
    reps = LANES // (QK_ROPE_DIM // 2)
    cos = jnp.concatenate([jnp.cos(ang_t)] * reps, axis=0).T
    sin = jnp.concatenate([jnp.sin(ang_t)] * reps, axis=0).T
    lane = lax.broadcasted_iota(jnp.int32, cos.shape, 1)
    first_half = (lane % QK_ROPE_DIM) < (QK_ROPE_DIM // 2)
    low = lane < QK_ROPE_DIM

    ql = _dot(hb, win_ref[:, _C_QLAT:_C_KVLAT])
    qn = (_rms(ql) * qg_ref[...]).astype(BF16)
    q = _dot(qn, wuq_ref[...]) * q_scale
    nope_w = N_HEADS * QK_NOPE_DIM
    for hp in range(N_HEADS // 2):
        qr = _rope_pairs(q[:, nope_w + LANES * hp: nope_w + LANES * (hp + 1)], cos, sin, first_half)
        qr_hi = pltpu.roll(qr, QK_ROPE_DIM, 1)
        for j, part in ((0, qr), (1, qr_hi)):
            hd = 2 * hp + j
            q_ref[0, hd, :, 0:LANES] = q[:, LANES * hd: LANES * (hd + 1)].astype(BF16)
            q_ref[0, hd, :, LANES:QK_PAD_DIM] = jnp.where(low, part, 0.0).astype(BF16)
    conv_stage(1)

    kvl = _dot(hb, win_ref[:, _C_KVLAT:_C_KR])
    kvn = (_rms(kvl) * kvg_ref[...]).astype(BF16)
    kn = _dot(kvn, wuk_ref[...])
    kr = _rope_pairs(_dot(hb, win_ref[:, _C_KR:_C_END]), cos, sin, first_half)
    kr = jnp.where(low, kr, 0.0).astype(BF16)
    for hd in range(N_HEADS):
        k_ref[0, hd, :, 0:LANES] = kn[:, LANES * hd: LANES * (hd + 1)].astype(BF16)
        k_ref[0, hd, :, LANES:QK_PAD_DIM] = kr
    conv_stage(2)

    vt = lax.dot_general(wuvt_ref[...], kvn, (((1,), (1,)), ((), ())), preferred_element_type=F32)
    for hd in range(N_HEADS):
        vt_ref[0, hd] = vt[V_HEAD_DIM * hd: V_HEAD_DIM * (hd + 1), :].astype(BF16)
    conv_stage(3)

    ya_ref[...] = _dot(act_ref[...], wco_ref[...]).astype(BF16)
    buf_ref[0, 0:CONV_HALO, :] = buf_ref[0, tm:tm + CONV_HALO, :]


def _in_conv(x2, mod3, g_pre, w_in_r, q_norm_g, w_uq_r, kv_norm_g, w_uk, w_uvt, pos_f, inv_freq,
             conv_w8, conv_b8, ln_g, ln_b, w_conv_out, batch, seq, tm, rows):
    nt = seq // tm
    q_scale = (1.0 / math.sqrt(QK_NOPE_DIM + QK_ROPE_DIM)) * math.log2(math.e)
    tok = lambda i: (i, 0)
    heads = lambda i: (i // nt, 0, i % nt, 0)
    return pl.pallas_call(
        functools.partial(_in_conv_kernel, q_scale=q_scale, tm=tm, nt=nt, rows=rows),
        grid=(batch * nt,),
        in_specs=[
            pl.BlockSpec((tm, D_MODEL), tok),
            pl.BlockSpec((1, 6, D_MODEL), lambda i: (i // nt, 0, 0)),
            _const_spec((1, D_MODEL)),
            _const_spec(w_in_r.shape),
            _const_spec((1, Q_LORA_RANK)),
            _const_spec(w_uq_r.shape),
            _const_spec((1, KV_LORA_RANK)),
            _const_spec(w_uk.shape),
            _const_spec(w_uvt.shape),
            pl.BlockSpec((1, 1, tm), lambda i: (i, 0, 0)),
            _const_spec((QK_ROPE_DIM // 2, 1)),
            _const_spec(conv_w8.shape),
            _const_spec((SUBLANES, CONV_DIM)),
            _const_spec((1, CONV_DIM)),
            _const_spec((1, CONV_DIM)),
            _const_spec(w_conv_out.shape),
        ],
        out_specs=[
            pl.BlockSpec((tm, D_MODEL), tok),
            pl.BlockSpec((tm, 2 * D_MODEL), tok),
            pl.BlockSpec((1, N_HEADS, tm, QK_PAD_DIM), heads),
            pl.BlockSpec((1, N_HEADS, tm, QK_PAD_DIM), heads),
            pl.BlockSpec((1, N_HEADS, V_HEAD_DIM, tm), lambda i: (i // nt, 0, 0, i % nt)),
        ],
        out_shape=[
            jax.ShapeDtypeStruct((batch * seq, D_MODEL), BF16),
            jax.ShapeDtypeStruct((batch * seq, 2 * D_MODEL), BF16),
            jax.ShapeDtypeStruct((batch, N_HEADS, seq, QK_PAD_DIM), BF16),
            jax.ShapeDtypeStruct((batch, N_HEADS, seq, QK_PAD_DIM), BF16),
            jax.ShapeDtypeStruct((batch, N_HEADS, V_HEAD_DIM, seq), BF16),
        ],
        scratch_shapes=[pltpu.VMEM((SUBLANES, CONV_HALO + tm, CONV_DIM), F32),
                        pltpu.VMEM((tm, CONV_DIM), BF16)],
        compiler_params=_params("arbitrary"),
        name="in_conv",
    )(x2, mod3, g_pre, w_in_r, q_norm_g, w_uq_r, kv_norm_g, w_uk, w_uvt,
      pos_f.reshape(batch * nt, 1, tm), inv_freq,
      conv_w8, conv_b8, ln_g, ln_b, w_conv_out)


def _attn_kernel(q_ref, k_ref, vt_ref, o_ref, m_ref, l_ref, acc_ref, s_ref, *, tq, tk_full, tk_diag, hpb):
    qi = pl.program_id(2)
    n_full = qi * (tq // tk_full)
    n_diag = tq // tk_diag
    diag_start = qi * tq
    m_ref[...] = jnp.full(m_ref.shape, -1e30, F32)
    l_ref[...] = jnp.zeros(l_ref.shape, F32)
    acc_ref[...] = jnp.zeros(acc_ref.shape, F32)

    def scores(hd, start, tk, q0=0):
        k = k_ref[0, hd, pl.ds(start, tk), :]
        s_ref[hd, :tk, q0:] = lax.dot_general(k, q_ref[0, hd, q0:, :], (((1,), (1,)), ((), ())),
                                              preferred_element_type=F32)

    def update(hd, start, tk, q0=0, diag=False):
        s = s_ref[hd, :tk, q0:]
        if diag:
            kpos = lax.broadcasted_iota(jnp.int32, (tk, tk), 0)
            qpos = lax.broadcasted_iota(jnp.int32, (tk, tk), 1)
            tri = jnp.where(qpos >= kpos, s[:, :tk], -1e30)
            s = tri if q0 + tk == tq else jnp.concatenate([tri, s[:, tk:]], axis=1)
        m_old = m_ref[hd, :, q0:]
        m_new = jnp.maximum(m_old, jnp.max(s, axis=0, keepdims=True))
        alpha = jnp.exp2(m_old - m_new)
        p = jnp.exp2(s - m_new)
        m_ref[hd, :, q0:] = m_new
        l_ref[hd, :, q0:] = alpha * l_ref[hd, :, q0:] + jnp.sum(p, axis=0, keepdims=True)
        vt = vt_ref[0, hd, :, pl.ds(start, tk)]
        acc_ref[hd, :, q0:] = alpha * acc_ref[hd, :, q0:] + _dot(vt, p.astype(BF16))

    def full_start(j):
        return pl.multiple_of(j * tk_full, tk_full)

    def diag_tile(d):
        return pl.multiple_of(diag_start + d * tk_diag, tk_diag), tk_diag, d * tk_diag

    if tk_full == tk_diag:
        scores(0, full_start(0), tk_full)
    else:
        pl.when(n_full > 0)(lambda: scores(0, full_start(0), tk_full))
        pl.when(n_full == 0)(lambda: scores(0, *diag_tile(0)))

    def full_round(j, next_scores):
        for hd in range(hpb):
            if hd + 1 < hpb:
                scores(hd + 1, full_start(j), tk_full)
            else:
                next_scores()
            update(hd, full_start(j), tk_full)

    def body(j, carry):
        full_round(j, lambda: scores(0, full_start(j + 1), tk_full))
        return carry

    lax.fori_loop(0, jnp.maximum(n_full - 1, 0), body, 0)
    pl.when(n_full > 0)(lambda: full_round(n_full - 1, lambda: scores(0, *diag_tile(0))))
    for d in range(n_diag):
        for hd in range(hpb):
            if hd + 1 < hpb:
                scores(hd + 1, *diag_tile(d))
            elif d + 1 < n_diag:
                scores(0, *diag_tile(d + 1))
            update(hd, *diag_tile(d), True)
    for hd in range(hpb):
        o = acc_ref[hd] / l_ref[hd]
        o_ref[0, :, V_HEAD_DIM * hd: V_HEAD_DIM * (hd + 1)] = o.T.astype(BF16)


def _attn(q4, k4, vt4, tq, tk_full, tk_diag, hpb):
    batch, heads, seq, _ = q4.shape
    return pl.pallas_call(
        functools.partial(_attn_kernel, tq=tq, tk_full=tk_full, tk_diag=tk_diag, hpb=hpb),
        grid=(batch, heads // hpb, seq // tq),
        in_specs=[
            pl.BlockSpec((1, hpb, tq, QK_PAD_DIM), lambda b, h, i: (b, h, i, 0)),
            pl.BlockSpec((1, hpb, seq, QK_PAD_DIM), lambda b, h, i: (b, h, 0, 0)),
            pl.BlockSpec((1, hpb, V_HEAD_DIM, seq), lambda b, h, i: (b, h, 0, 0)),
        ],
        out_specs=pl.BlockSpec((1, tq, hpb * V_HEAD_DIM), lambda b, h, i: (b, i, h)),
        out_shape=jax.ShapeDtypeStruct((batch, seq, heads * V_HEAD_DIM), BF16),
        scratch_shapes=[pltpu.VMEM((hpb, 1, tq), F32), pltpu.VMEM((hpb, 1, tq), F32),
                        pltpu.VMEM((hpb, V_HEAD_DIM, tq), F32),
                        pltpu.VMEM((hpb, max(tk_full, tk_diag), tq), F32)],
        compiler_params=_params("arbitrary", "arbitrary", "arbitrary"),
        name="attn",
    )(q4, k4, vt4)


def _mix_out_kernel(o_ref, ya_ref, gate_ref, x_ref, mod_ref, gpost_ref, wao_ref, wout_ref, x1_ref):
    y_b = _dot(o_ref[...], wao_ref[...])
    g_a = gate_ref[:, :D_MODEL].astype(F32)
    g_b = gate_ref[:, D_MODEL:].astype(F32)
    merged = g_a * ya_ref[...].astype(F32) + g_b * y_b
    y = _dot(merged.astype(BF16), wout_ref[...])
    gate1 = mod_ref[0, 2:3, :]
    x1_ref[...] = x_ref[...] + gate1 * (_rms(y) * gpost_ref[...])


def _mix_out(o2, ya2, gates, x2, mod3, g_post, w_attn_out, w_out, seq, tm):
    tokens = x2.shape[0]
    nt = seq // tm
    tok = lambda i: (i, 0)
    return pl.pallas_call(
        _mix_out_kernel,
        grid=(tokens // tm,),
        in_specs=[
            pl.BlockSpec((tm, D_MODEL), tok),
            pl.BlockSpec((tm, D_MODEL), tok),
            pl.BlockSpec((tm, 2 * D_MODEL), tok),
            pl.BlockSpec((tm, D_MODEL), tok),
            pl.BlockSpec((1, 6, D_MODEL), lambda i: (i // nt, 0, 0)),
            _const_spec((1, D_MODEL)),
            _const_spec(w_attn_out.shape),
            _const_spec(w_out.shape),
        ],
        out_specs=pl.BlockSpec((tm, D_MODEL), tok),
        out_shape=jax.ShapeDtypeStruct((tokens, D_MODEL), F32),
        compiler_params=_params("arbitrary"),
        name="mix_out",
    )(o2, ya2, gates, x2, mod3, g_post, w_attn_out, w_out)


def _mlp_kernel(x_ref, mod_ref, gpre_ref, gpost_ref, w1_ref, w2_ref, o_ref):
    x = x_ref[...]
    shift = mod_ref[0, 3:4, :]
    scale = mod_ref[0, 4:5, :]
    gate = mod_ref[0, 5:6, :]
    h = (_rms(x) * gpre_ref[...]) * (1.0 + scale) + shift
    hid = jnp.maximum(_dot(h.astype(BF16), w1_ref[...]), 0.0)
    y = _dot((hid * hid).astype(BF16), w2_ref[...])
    o_ref[...] = x + gate * (_rms(y) * gpost_ref[...])


def _mlp(x1, mod3, g_pre, g_post, w1, w2, seq, tm):
    tokens = x1.shape[0]
    nt = seq // tm
    tok = lambda i: (i, 0)
    return pl.pallas_call(
        _mlp_kernel,
        grid=(tokens // tm,),
        in_specs=[
            pl.BlockSpec((tm, D_MODEL), tok),
            pl.BlockSpec((1, 6, D_MODEL), lambda i: (i // nt, 0, 0)),
            _const_spec((1, D_MODEL)),
            _const_spec((1, D_MODEL)),
            _const_spec(w1.shape),
            _const_spec(w2.shape),
        ],
        out_specs=pl.BlockSpec((tm, D_MODEL), tok),
        out_shape=jax.ShapeDtypeStruct((tokens, D_MODEL), F32),
        compiler_params=_params("arbitrary"),
        name="mlp",
    )(x1, mod3, g_pre, g_post, w1, w2)


def _head_major_cols(w, split):
    k = w.shape[0]
    w3 = w.reshape(k, N_HEADS, -1)
    return jnp.concatenate([w3[:, :, :split].reshape(k, -1), w3[:, :, split:].reshape(k, -1)], axis=1)


def kernel(x, c, positions, w_ada, b_ada, g_pre_mix, g_post_mix, g_pre_mlp, g_post_mlp, w_in, conv_w, conv_b, conv_norm_g, conv_norm_b, w_conv_out, q_norm_g, w_uq, kv_norm_g, w_ukv, w_attn_out, w_out, w_mlp_in, w_mlp_out):
    batch, seq, d = x.shape
    depth = w_ada.shape[0]
    tokens = batch * seq
    row = lambda v: v.reshape(1, -1)

    inv_freq = 1.0 / (ROPE_THETA ** (jnp.arange(0, QK_ROPE_DIM, 2, dtype=F32) / QK_ROPE_DIM))
    inv_freq = inv_freq.reshape(-1, 1)
    pos_f = positions.astype(F32)
    c_pad = jnp.pad(c, ((0, 8 - batch), (0, 0)))

    x2 = x.reshape(tokens, d)
    for l in range(depth):
        w_in_r = _prep_w_in(w_in[l])
        w_uq_r = _head_major_cols(w_uq[l], QK_NOPE_DIM).astype(BF16)
        w_ukv_r = _head_major_cols(w_ukv[l], QK_NOPE_DIM).astype(BF16)
        w_uk = w_ukv_r[:, :N_HEADS * QK_NOPE_DIM]
        w_uvt = w_ukv_r[:, N_HEADS * QK_NOPE_DIM:].T

        mod = _ada(c_pad, w_ada[l], row(b_ada[l]))
        mod3 = mod[:batch].reshape(batch, 6, d)

        conv_w8 = jnp.broadcast_to(conv_w[l][:, None, :], (CONV_KERNEL, SUBLANES, CONV_DIM))
        conv_b8 = jnp.broadcast_to(row(conv_b[l]), (SUBLANES, CONV_DIM))
        y_a, gates, q4, k4, vt4 = _in_conv(x2, mod3, row(g_pre_mix[l]), w_in_r, row(q_norm_g[l]), w_uq_r,
                                           row(kv_norm_g[l]), w_uk, w_uvt, pos_f, inv_freq, conv_w8, conv_b8,
                                           row(conv_norm_g[l]), row(conv_norm_b[l]), w_conv_out[l].astype(BF16),
                                           batch, seq, tm=256, rows=32)
        o = _attn(q4, k4, vt4, tq=1024, tk_full=512, tk_diag=512, hpb=4)
        x2 = _mix_out(o.reshape(tokens, d), y_a, gates, x2, mod3, row(g_post_mix[l]),
                      w_attn_out[l].astype(BF16), w_out[l].astype(BF16), seq, tm=512)
        x2 = _mlp(x2, mod3, row(g_pre_mlp[l]), row(g_post_mlp[l]), w_mlp_in[l].astype(BF16),
                  w_mlp_out[l].astype(BF16), seq, tm=512)
    return x2.reshape(batch, seq, d)
```

```python
import functools
import math

import jax
import jax.numpy as jnp
from jax import lax
from jax.experimental import pallas as pl
from jax.experimental.pallas import tpu as pltpu

D_MODEL = 1024
CONV_DIM = 1024
CONV_KERNEL = 31
N_HEADS = 8
QK_NOPE_DIM = 128
QK_ROPE_DIM = 64
V_HEAD_DIM = 128
Q_LORA_RANK = 384
KV_LORA_RANK = 256
ROPE_THETA = 10000.0
D_FF = 4 * D_MODEL
EPS = 1e-6

LANES = 128
SUBLANES = 8
QK_PAD_DIM = 2 * LANES
CONV_HALO = 32
VMEM_LIMIT = 56 * 1024 * 1024

_C_GLU = 0
_C_GATE = 2 * CONV_DIM
_C_QLAT = _C_GATE + 2 * D_MODEL
_C_KVLAT = _C_QLAT + Q_LORA_RANK
_C_KR = _C_KVLAT + KV_LORA_RANK
_C_END = _C_KR + LANES

BF16 = jnp.bfloat16
F32 = jnp.float32


def _dot(a, b):
    return jnp.dot(a, b, preferred_element_type=F32)


def _rms(v):
    return v * lax.rsqrt(jnp.mean(v * v, axis=-1, keepdims=True) + EPS)


def _params(*sem):
    return pltpu.CompilerParams(dimension_semantics=sem, vmem_limit_bytes=VMEM_LIMIT)


def _const_spec(shape):
    nd = len(shape)
    return pl.BlockSpec(shape, lambda *_: (0,) * nd, pipeline_mode=pl.Buffered(1))


def _ada_kernel(c_ref, w_ref, b_ref, o_ref):
    c = c_ref[...]
    act = c * jax.nn.sigmoid(c)
    o_ref[...] = _dot(act.astype(BF16), w_ref[...].astype(BF16)) + b_ref[...]


def _ada(c_pad, w_ada, b_ada):
    rows, d = c_pad.shape
    n = w_ada.shape[1]
    tn = 1024
    return pl.pallas_call(
        _ada_kernel,
        grid=(n // tn,),
        in_specs=[
            pl.BlockSpec((rows, d), lambda j: (0, 0)),
            pl.BlockSpec((d, tn), lambda j: (0, j)),
            pl.BlockSpec((1, tn), lambda j: (0, j)),
        ],
        out_specs=pl.BlockSpec((rows, tn), lambda j: (0, j)),
        out_shape=jax.ShapeDtypeStruct((rows, n), F32),
        compiler_params=_params("arbitrary"),
        name="ada",
    )(c_pad, w_ada, b_ada)


_S_QLAT = 2 * CONV_DIM
_S_KVLAT = _S_QLAT + Q_LORA_RANK
_S_KR = _S_KVLAT + KV_LORA_RANK
_S_GATE = _S_KR + QK_ROPE_DIM


def _prep_w_in_kernel(wt_ref, o_ref):
    def move(dst, src, n):
        for off in range(0, n, QK_PAD_DIM):
            m = min(QK_PAD_DIM, n - off)
            o_ref[:, dst + off:dst + off + m] = wt_ref[src + off:src + off + m, :].T.astype(BF16)

    move(_C_GLU, 0, _S_QLAT)
    move(_C_GATE, _S_GATE, _C_QLAT - _C_GATE)
    move(_C_QLAT, _S_QLAT, Q_LORA_RANK)
    move(_C_KVLAT, _S_KVLAT, KV_LORA_RANK)
    kr = wt_ref[_S_KR:_S_KR + LANES, :].T
    lane = lax.broadcasted_iota(jnp.int32, kr.shape, 1)
    o_ref[:, _C_KR:_C_END] = jnp.where(lane < QK_ROPE_DIM, kr, pltpu.roll(kr, QK_ROPE_DIM, 1)).astype(BF16)


def _prep_w_in(w):
    k, n = w.shape
    return pl.pallas_call(
        _prep_w_in_kernel,
        grid=(1,),
        in_specs=[_const_spec((n, k))],
        out_specs=pl.BlockSpec((k, _C_END), lambda i: (0, 0)),
        out_shape=jax.ShapeDtypeStruct((k, _C_END), BF16),
        compiler_params=_params("arbitrary"),
        name="prep_w_in",
    )(w.T)


def _rope_pairs(t, cos, sin, first_half):
    rot = jnp.where(first_half, -pltpu.roll(t, LANES - QK_ROPE_DIM // 2, 1),
                    pltpu.roll(t, QK_ROPE_DIM // 2, 1))
    return t * cos + rot * sin


def _in_conv_kernel(x_ref, mod_ref, gpre_ref, win_ref, qg_ref, wuq_ref, kvg_ref, wuk_ref, wuvt_ref,
                    pos_ref, invf_ref, cw_ref, cb_ref, lng_ref, lnb_ref, wco_ref,
                    ya_ref, gate_ref, q_ref, k_ref, vt_ref, buf_ref, act_ref, *, q_scale, tm, nt, rows):
    t = pl.program_id(0) % nt

    @pl.when(t == 0)
    def _():
        buf_ref[0, 0:CONV_HALO, :] = jnp.zeros((CONV_HALO, CONV_DIM), F32)

    x = x_ref[...]
    shift = mod_ref[0, 0:1, :]
    scale = mod_ref[0, 1:2, :]
    h = (_rms(x) * gpre_ref[...]) * (1.0 + scale) + shift
    hb = h.astype(BF16)

    zg = _dot(hb, win_ref[:, _C_GLU:_C_GATE])
    buf_ref[0, CONV_HALO:, :] = zg[:, :CONV_DIM] * jax.nn.sigmoid(zg[:, CONV_DIM:])
    n_shift = tm + CONV_HALO - SUBLANES
    for b in range(1, SUBLANES):
        buf_ref[b, 0:n_shift, :] = buf_ref[0, b:b + n_shift, :]
    first = CONV_HALO - (CONV_KERNEL - 1)
    groups = rows // SUBLANES
    half = CONV_DIM // 2

    def conv_chunk(r0):
        halves = []
        for c0 in (0, half):
            acc = jnp.broadcast_to(cb_ref[:, c0:c0 + half][None], (groups, SUBLANES, half))
            for k in range(CONV_KERNEL):
                a, b = divmod(first + k, SUBLANES)
                win = buf_ref[b, r0 + SUBLANES * a: r0 + SUBLANES * a + rows, c0:c0 + half]
                acc = acc + cw_ref[k, :, c0:c0 + half][None] * win.reshape(groups, SUBLANES, half)
            halves.append(acc.reshape(rows, half))
        acc = jnp.concatenate(halves, axis=-1)
        mu = jnp.mean(acc, axis=-1, keepdims=True)
        cen = acc - mu
        var = jnp.mean(cen * cen, axis=-1, keepdims=True)
        y = cen * lax.rsqrt(var + EPS) * lng_ref[...] + lnb_ref[...]
        y = y * jax.nn.sigmoid(y)
        act_ref[r0:r0 + rows, :] = y.astype(BF16)

    chunk_starts = list(range(0, tm, rows))
    per_stage = -(-len(chunk_starts) // 4)

    def conv_stage(i):
        for r0 in chunk_starts[i * per_stage:(i + 1) * per_stage]:
            conv_chunk(r0)

    gate_ref[...] = jax.nn.sigmoid(_dot(hb, win_ref[:, _C_GATE:_C_QLAT])).astype(BF16)
    conv_stage(0)

    ang_t = invf_ref[...] * pos_ref[0]
    reps = LANES // (QK_ROPE_DIM // 2)
    cos = jnp.concatenate([jnp.cos(ang_t)] * reps, axis=0).T
    sin = jnp.concatenate([jnp.sin(ang_t)] * reps, axis=0).T
    lane = lax.broadcasted_iota(jnp.int32, cos.shape, 1)
    first_half = (lane % QK_ROPE_DIM) < (QK_ROPE_DIM // 2)
    low = lane < QK_ROPE_DIM

    ql = _dot(hb, win_ref[:, _C_QLAT:_C_KVLAT])
    qn = (_rms(ql) * qg_ref[...]).astype(BF16)
    q = _dot(qn, wuq_ref[...]) * q_scale
    nope_w = N_HEADS * QK_NOPE_DIM
    for hp in range(N_HEADS // 2):
        qr = _rope_pairs(q[:, nope_w + LANES * hp: nope_w + LANES * (hp + 1)], cos, sin, first_half)
        qr_hi = pltpu.roll(qr, QK_ROPE_DIM, 1)
        for j, part in ((0, qr), (1, qr_hi)):
            hd = 2 * hp + j
            q_ref[0, hd, :, 0:LANES] = q[:, LANES * hd: LANES * (hd + 1)].astype(BF16)
            q_ref[0, hd, :, LANES:QK_PAD_DIM] = jnp.where(low, part, 0.0).astype(BF16)
    conv_stage(1)

    kvl = _dot(hb, win_ref[:, _C_KVLAT:_C_KR])
    kvn = (_rms(kvl) * kvg_ref[...]).astype(BF16)
    kn = _dot(kvn, wuk_ref[...])
    kr = _rope_pairs(_dot(hb, win_ref[:, _C_KR:_C_END]), cos, sin, first_half)
    kr = jnp.where(low, kr, 0.0).astype(BF16)
    for hd in range(N_HEADS):
        k_ref[0, hd, :, 0:LANES] = kn[:, LANES * hd: LANES * (hd + 1)].astype(BF16)
        k_ref[0, hd, :, LANES:QK_PAD_DIM] = kr
    conv_stage(2)

    vt = lax.dot_general(wuvt_ref[...], kvn, (((1,), (1,)), ((), ())), preferred_element_type=F32)
    for hd in range(N_HEADS):
        vt_ref[0, hd] = vt[V_HEAD_DIM * hd: V_HEAD_DIM * (hd + 1), :].astype(BF16)
    conv_stage(3)

    ya_ref[...] = _dot(act_ref[...], wco_ref[...]).astype(BF16)
    buf_ref[0, 0:CONV_HALO, :] = buf_ref[0, tm:tm + CONV_HALO, :]


def _in_conv(x2, mod3, g_pre, w_in_r, q_norm_g, w_uq_r, kv_norm_g, w_uk, w_uvt, pos_f, inv_freq,
             conv_w8, conv_b8, ln_g, ln_b, w_conv_out, batch, seq, tm, rows):
    nt = seq // tm
    q_scale = (1.0 / math.sqrt(QK_NOPE_DIM + QK_ROPE_DIM)) * math.log2(math.e)
    tok = lambda i: (i, 0)
    heads = lambda i: (i // nt, 0, i % nt, 0)
    return pl.pallas_call(
        functools.partial(_in_conv_kernel, q_scale=q_scale, tm=tm, nt=nt, rows=rows),
        grid=(batch * nt,),
        in_specs=[
            pl.BlockSpec((tm, D_MODEL), tok),
            pl.BlockSpec((1, 6, D_MODEL), lambda i: (i // nt, 0, 0)),
            _const_spec((1, D_MODEL)),
            _const_spec(w_in_r.shape),
            _const_spec((1, Q_LORA_RANK)),
            _const_spec(w_uq_r.shape),
            _const_spec((1, KV_LORA_RANK)),
            _const_spec(w_uk.shape),
            _const_spec(w_uvt.shape),
            pl.BlockSpec((1, 1, tm), lambda i: (i, 0, 0)),
            _const_spec((QK_ROPE_DIM // 2, 1)),
            _const_spec(conv_w8.shape),
            _const_spec((SUBLANES, CONV_DIM)),
            _const_spec((1, CONV_DIM)),
            _const_spec((1, CONV_DIM)),
            _const_spec(w_conv_out.shape),
        ],
        out_specs=[
            pl.BlockSpec((tm, D_MODEL), tok),
            pl.BlockSpec((tm, 2 * D_MODEL), tok),
            pl.BlockSpec((1, N_HEADS, tm, QK_PAD_DIM), heads),
            pl.BlockSpec((1, N_HEADS, tm, QK_PAD_DIM), heads),
            pl.BlockSpec((1, N_HEADS, V_HEAD_DIM, tm), lambda i: (i // nt, 0, 0, i % nt)),
        ],
        out_shape=[
            jax.ShapeDtypeStruct((batch * seq, D_MODEL), BF16),
            jax.ShapeDtypeStruct((batch * seq, 2 * D_MODEL), BF16),
            jax.ShapeDtypeStruct((batch, N_HEADS, seq, QK_PAD_DIM), BF16),
            jax.ShapeDtypeStruct((batch, N_HEADS, seq, QK_PAD_DIM), BF16),
            jax.ShapeDtypeStruct((batch, N_HEADS, V_HEAD_DIM, seq), BF16),
        ],
        scratch_shapes=[pltpu.VMEM((SUBLANES, CONV_HALO + tm, CONV_DIM), F32),
                        pltpu.VMEM((tm, CONV_DIM), BF16)],
        compiler_params=_params("arbitrary"),
        name="in_conv",
    )(x2, mod3, g_pre, w_in_r, q_norm_g, w_uq_r, kv_norm_g, w_uk, w_uvt,
      pos_f.reshape(batch * nt, 1, tm), inv_freq,
      conv_w8, conv_b8, ln_g, ln_b, w_conv_out)


def _attn_kernel(q_ref, k_ref, vt_ref, *rest, tq, tk_full, tk_diag, hpb, n_weights):
    w_refs, o_ref, wb_refs = rest[:n_weights], rest[n_weights], rest[n_weights + 1:2 * n_weights + 1]
    m_ref, l_ref, acc_ref, s_ref = rest[2 * n_weights + 1:]
    for w_ref, wb_ref in zip(w_refs, wb_refs):
        wb_ref[...] = w_ref[...].astype(BF16)
    qi = pl.program_id(2)
    n_full = qi * (tq // tk_full)
    n_diag = tq // tk_diag
    diag_start = qi * tq
    m_ref[...] = jnp.full(m_ref.shape, -1e30, F32)
    l_ref[...] = jnp.zeros(l_ref.shape, F32)
    acc_ref[...] = jnp.zeros(acc_ref.shape, F32)

    def scores(hd, start, tk, q0=0):
        k = k_ref[0, hd, pl.ds(start, tk), :]
        s_ref[hd, :tk, q0:] = lax.dot_general(k, q_ref[0, hd, q0:, :], (((1,), (1,)), ((), ())),
                                              preferred_element_type=F32)

    def update(hd, start, tk, q0=0, diag=False):
        s = s_ref[hd, :tk, q0:]
        if diag:
            kpos = lax.broadcasted_iota(jnp.int32, (tk, tk), 0)
            qpos = lax.broadcasted_iota(jnp.int32, (tk, tk), 1)
            tri = jnp.where(qpos >= kpos, s[:, :tk], -1e30)
            s = tri if q0 + tk == tq else jnp.concatenate([tri, s[:, tk:]], axis=1)
        m_old = m_ref[hd, :, q0:]
        m_new = jnp.maximum(m_old, jnp.max(s, axis=0, keepdims=True))
        alpha = jnp.exp2(m_old - m_new)
        p = jnp.exp2(s - m_new)
        m_ref[hd, :, q0:] = m_new
        l_ref[hd, :, q0:] = alpha * l_ref[hd, :, q0:] + jnp.sum(p, axis=0, keepdims=True)
        vt = vt_ref[0, hd, :, pl.ds(start, tk)]
        acc_ref[hd, :, q0:] = alpha * acc_ref[hd, :, q0:] + _dot(vt, p.astype(BF16))

    def full_start(j):
        return pl.multiple_of(j * tk_full, tk_full)

    def diag_tile(d):
        return pl.multiple_of(diag_start + d * tk_diag, tk_diag), tk_diag, d * tk_diag

    if tk_full == tk_diag:
        scores(0, full_start(0), tk_full)
    else:
        pl.when(n_full > 0)(lambda: scores(0, full_start(0), tk_full))
        pl.when(n_full == 0)(lambda: scores(0, *diag_tile(0)))

    def full_round(j, next_scores):
        for hd in range(hpb):
            if hd + 1 < hpb:
                scores(hd + 1, full_start(j), tk_full)
            else:
                next_scores()
            update(hd, full_start(j), tk_full)

    def body(j, carry):
        full_round(j, lambda: scores(0, full_start(j + 1), tk_full))
        return carry

    lax.fori_loop(0, jnp.maximum(n_full - 1, 0), body, 0)
    pl.when(n_full > 0)(lambda: full_round(n_full - 1, lambda: scores(0, *diag_tile(0))))
    for d in range(n_diag):
        for hd in range(hpb):
            if hd + 1 < hpb:
                scores(hd + 1, *diag_tile(d))
            elif d + 1 < n_diag:
                scores(0, *diag_tile(d + 1))
            update(hd, *diag_tile(d), True)
    for hd in range(hpb):
        o = acc_ref[hd] / l_ref[hd]
        o_ref[0, :, V_HEAD_DIM * hd: V_HEAD_DIM * (hd + 1)] = o.T.astype(BF16)


def _attn(q4, k4, vt4, later_weights, tq, tk_full, tk_diag, hpb):
    batch, heads, seq, _ = q4.shape
    grid = (batch, heads // hpb, seq // tq)
    steps = grid[0] * grid[1] * grid[2]
    slab = lambda b, h, i: ((b * grid[1] + h) * grid[2] + i, 0)
    w_specs = [pl.BlockSpec((w.shape[0] // steps, w.shape[1]), slab) for w in later_weights]
    outs = pl.pallas_call(
        functools.partial(_attn_kernel, tq=tq, tk_full=tk_full, tk_diag=tk_diag, hpb=hpb,
                          n_weights=len(later_weights)),
        grid=grid,
        in_specs=[
            pl.BlockSpec((1, hpb, tq, QK_PAD_DIM), lambda b, h, i: (b, h, i, 0)),
            pl.BlockSpec((1, hpb, seq, QK_PAD_DIM), lambda b, h, i: (b, h, 0, 0)),
            pl.BlockSpec((1, hpb, V_HEAD_DIM, seq), lambda b, h, i: (b, h, 0, 0)),
        ] + w_specs,
        out_specs=[pl.BlockSpec((1, tq, hpb * V_HEAD_DIM), lambda b, h, i: (b, i, h))] + w_specs,
        out_shape=[jax.ShapeDtypeStruct((batch, seq, heads * V_HEAD_DIM), BF16)]
        + [jax.ShapeDtypeStruct(w.shape, BF16) for w in later_weights],
        scratch_shapes=[pltpu.VMEM((hpb, 1, tq), F32), pltpu.VMEM((hpb, 1, tq), F32),
                        pltpu.VMEM((hpb, V_HEAD_DIM, tq), F32),
                        pltpu.VMEM((hpb, max(tk_full, tk_diag), tq), F32)],
        compiler_params=_params("arbitrary", "arbitrary", "arbitrary"),
        name="attn",
    )(q4, k4, vt4, *later_weights)
    return outs[0], outs[1:]


def _mix_out_kernel(o_ref, ya_ref, gate_ref, x_ref, mod_ref, gpost_ref, wao_ref, wout_ref, x1_ref):
    y_b = _dot(o_ref[...], wao_ref[...])
    g_a = gate_ref[:, :D_MODEL].astype(F32)
    g_b = gate_ref[:, D_MODEL:].astype(F32)
    merged = g_a * ya_ref[...].astype(F32) + g_b * y_b
    y = _dot(merged.astype(BF16), wout_ref[...])
    gate1 = mod_ref[0, 2:3, :]
    x1_ref[...] = x_ref[...] + gate1 * (_rms(y) * gpost_ref[...])


def _mix_out(o2, ya2, gates, x2, mod3, g_post, w_attn_out, w_out, seq, tm):
    tokens = x2.shape[0]
    nt = seq // tm
    tok = lambda i: (i, 0)
    return pl.pallas_call(
        _mix_out_kernel,
        grid=(tokens // tm,),
        in_specs=[
            pl.BlockSpec((tm, D_MODEL), tok),
            pl.BlockSpec((tm, D_MODEL), tok),
            pl.BlockSpec((tm, 2 * D_MODEL), tok),
            pl.BlockSpec((tm, D_MODEL), tok),
            pl.BlockSpec((1, 6, D_MODEL), lambda i: (i // nt, 0, 0)),
            _const_spec((1, D_MODEL)),
            _const_spec(w_attn_out.shape),
            _const_spec(w_out.shape),
        ],
        out_specs=pl.BlockSpec((tm, D_MODEL), tok),
        out_shape=jax.ShapeDtypeStruct((tokens, D_MODEL), F32),
        compiler_params=_params("arbitrary"),
        name="mix_out",
    )(o2, ya2, gates, x2, mod3, g_post, w_attn_out, w_out)


def _mlp_kernel(x_ref, mod_ref, gpre_ref, gpost_ref, w1_ref, w2_ref, o_ref):
    x = x_ref[...]
    shift = mod_ref[0, 3:4, :]
    scale = mod_ref[0, 4:5, :]
    gate = mod_ref[0, 5:6, :]
    h = (_rms(x) * gpre_ref[...]) * (1.0 + scale) + shift
    hid = jnp.maximum(_dot(h.astype(BF16), w1_ref[...]), 0.0)
    y = _dot((hid * hid).astype(BF16), w2_ref[...])
    o_ref[...] = x + gate * (_rms(y) * gpost_ref[...])


def _mlp(x1, mod3, g_pre, g_post, w1, w2, seq, tm):
    tokens = x1.shape[0]
    nt = seq // tm
    tok = lambda i: (i, 0)
    return pl.pallas_call(
        _mlp_kernel,
        grid=(tokens // tm,),
        in_specs=[
            pl.BlockSpec((tm, D_MODEL), tok),
            pl.BlockSpec((1, 6, D_MODEL), lambda i: (i // nt, 0, 0)),
            _const_spec((1, D_MODEL)),
            _const_spec((1, D_MODEL)),
            _const_spec(w1.shape),
            _const_spec(w2.shape),
        ],
        out_specs=pl.BlockSpec((tm, D_MODEL), tok),
        out_shape=jax.ShapeDtypeStruct((tokens, D_MODEL), F32),
        compiler_params=_params("arbitrary"),
        name="mlp",
    )(x1, mod3, g_pre, g_post, w1, w2)


def _head_major_cols(w, split):
    k = w.shape[0]
    w3 = w.reshape(k, N_HEADS, -1)
    return jnp.concatenate([w3[:, :, :split].reshape(k, -1), w3[:, :, split:].reshape(k, -1)], axis=1)


def kernel(x, c, positions, w_ada, b_ada, g_pre_mix, g_post_mix, g_pre_mlp, g_post_mlp, w_in, conv_w, conv_b, conv_norm_g, conv_norm_b, w_conv_out, q_norm_g, w_uq, kv_norm_g, w_ukv, w_attn_out, w_out, w_mlp_in, w_mlp_out):
    batch, seq, d = x.shape
    depth = w_ada.shape[0]
    tokens = batch * seq
    row = lambda v: v.reshape(1, -1)

    inv_freq = 1.0 / (ROPE_THETA ** (jnp.arange(0, QK_ROPE_DIM, 2, dtype=F32) / QK_ROPE_DIM))
    inv_freq = inv_freq.reshape(-1, 1)
    pos_f = positions.astype(F32)
    c_pad = jnp.pad(c, ((0, 8 - batch), (0, 0)))

    x2 = x.reshape(tokens, d)
    for l in range(depth):
        w_in_r = _prep_w_in(w_in[l])
        w_uq_r = _head_major_cols(w_uq[l], QK_NOPE_DIM).astype(BF16)
        w_ukv_r = _head_major_cols(w_ukv[l], QK_NOPE_DIM).astype(BF16)
        w_uk = w_ukv_r[:, :N_HEADS * QK_NOPE_DIM]
        w_uvt = w_ukv_r[:, N_HEADS * QK_NOPE_DIM:].T

        mod = _ada(c_pad, w_ada[l], row(b_ada[l]))
        mod3 = mod[:batch].reshape(batch, 6, d)

        conv_w8 = jnp.broadcast_to(conv_w[l][:, None, :], (CONV_KERNEL, SUBLANES, CONV_DIM))
        conv_b8 = jnp.broadcast_to(row(conv_b[l]), (SUBLANES, CONV_DIM))
        y_a, gates, q4, k4, vt4 = _in_conv(x2, mod3, row(g_pre_mix[l]), w_in_r, row(q_norm_g[l]), w_uq_r,
                                           row(kv_norm_g[l]), w_uk, w_uvt, pos_f, inv_freq, conv_w8, conv_b8,
                                           row(conv_norm_g[l]), row(conv_norm_b[l]), w_conv_out[l].astype(BF16),
                                           batch, seq, tm=256, rows=32)
        o, (w_ao, w_o, w_1, w_2) = _attn(q4, k4, vt4, (w_attn_out[l], w_out[l], w_mlp_in[l], w_mlp_out[l]),
                                         tq=1024, tk_full=512, tk_diag=512, hpb=4)
        x2 = _mix_out(o.reshape(tokens, d), y_a, gates, x2, mod3, row(g_post_mix[l]), w_ao, w_o, seq, tm=512)
        x2 = _mlp(x2, mod3, row(g_pre_mlp[l]), row(g_post_mlp[l]), w_1, w_2, seq, tm=512)
    return x2.reshape(batch, seq, d)
```

```python
import functools
import math

import jax
import jax.numpy as jnp
from jax import lax
from jax.experimental import pallas as pl
from jax.experimental.pallas import tpu as pltpu

D_MODEL = 1024
CONV_DIM = 1024
CONV_KERNEL = 31
N_HEADS = 8
QK_NOPE_DIM = 128
QK_ROPE_DIM = 64
V_HEAD_DIM = 128
Q_LORA_RANK = 384
KV_LORA_RANK = 256
ROPE_THETA = 10000.0
D_FF = 4 * D_MODEL
EPS = 1e-6

LANES = 128
SUBLANES = 8
QK_PAD_DIM = 2 * LANES
CONV_HALO = 32
VMEM_LIMIT = 56 * 1024 * 1024

_C_GLU = 0
_C_GATE = 2 * CONV_DIM
_C_QLAT = _C_GATE + 2 * D_MODEL
_C_KVLAT = _C_QLAT + Q_LORA_RANK
_C_KR = _C_KVLAT + KV_LORA_RANK
_C_END = _C_KR + LANES

BF16 = jnp.bfloat16
F32 = jnp.float32


def _dot(a, b):
    return jnp.dot(a, b, preferred_element_type=F32)


def _rms(v):
    return v * lax.rsqrt(jnp.mean(v * v, axis=-1, keepdims=True) + EPS)


def _params(*sem):
    return pltpu.CompilerParams(dimension_semantics=sem, vmem_limit_bytes=VMEM_LIMIT)


def _const_spec(shape):
    nd = len(shape)
    return pl.BlockSpec(shape, lambda *_: (0,) * nd, pipeline_mode=pl.Buffered(1))


def _ada_kernel(c_ref, w_ref, b_ref, o_ref):
    c = c_ref[...]
    act = c * jax.nn.sigmoid(c)
    o_ref[...] = _dot(act.astype(BF16), w_ref[...].astype(BF16)) + b_ref[...]


def _ada(c_pad, w_ada, b_ada):
    rows, d = c_pad.shape
    n = w_ada.shape[1]
    tn = 1024
    return pl.pallas_call(
        _ada_kernel,
        grid=(n // tn,),
        in_specs=[
            pl.BlockSpec((rows, d), lambda j: (0, 0)),
            pl.BlockSpec((d, tn), lambda j: (0, j)),
            pl.BlockSpec((1, tn), lambda j: (0, j)),
        ],
        out_specs=pl.BlockSpec((rows, tn), lambda j: (0, j)),
        out_shape=jax.ShapeDtypeStruct((rows, n), F32),
        compiler_params=_params("arbitrary"),
        name="ada",
    )(c_pad, w_ada, b_ada)


_S_QLAT = 2 * CONV_DIM
_S_KVLAT = _S_QLAT + Q_LORA_RANK
_S_KR = _S_KVLAT + KV_LORA_RANK
_S_GATE = _S_KR + QK_ROPE_DIM


def _prep_w_in_kernel(wt_ref, o_ref):
    def move(dst, src, n):
        for off in range(0, n, QK_PAD_DIM):
            m = min(QK_PAD_DIM, n - off)
            o_ref[:, dst + off:dst + off + m] = wt_ref[src + off:src + off + m, :].T.astype(BF16)

    move(_C_GLU, 0, _S_QLAT)
    move(_C_GATE, _S_GATE, _C_QLAT - _C_GATE)
    move(_C_QLAT, _S_QLAT, Q_LORA_RANK)
    move(_C_KVLAT, _S_KVLAT, KV_LORA_RANK)
    kr = wt_ref[_S_KR:_S_KR + LANES, :].T
    lane = lax.broadcasted_iota(jnp.int32, kr.shape, 1)
    o_ref[:, _C_KR:_C_END] = jnp.where(lane < QK_ROPE_DIM, kr, pltpu.roll(kr, QK_ROPE_DIM, 1)).astype(BF16)


def _prep_w_in(w):
    k, n = w.shape
    return pl.pallas_call(
        _prep_w_in_kernel,
        grid=(1,),
        in_specs=[_const_spec((n, k))],
        out_specs=pl.BlockSpec((k, _C_END), lambda i: (0, 0)),
        out_shape=jax.ShapeDtypeStruct((k, _C_END), BF16),
        compiler_params=_params("arbitrary"),
        name="prep_w_in",
    )(w.T)


def _rope_pairs(t, cos, sin, first_half):
    rot = jnp.where(first_half, -pltpu.roll(t, LANES - QK_ROPE_DIM // 2, 1),
                    pltpu.roll(t, QK_ROPE_DIM // 2, 1))
    return t * cos + rot * sin


def _in_conv_kernel(x_ref, mod_ref, gpre_ref, win_ref, qg_ref, wuq_ref, kvg_ref, wuk_ref, wuvt_ref,
                    pos_ref, invf_ref, cw_ref, cb_ref, lng_ref, lnb_ref, wco_ref,
                    ya_ref, gate_ref, q_ref, k_ref, vt_ref, buf_ref, act_ref, *, q_scale, tm, nt, rows):
    t = pl.program_id(0) % nt

    @pl.when(t == 0)
    def _():
        buf_ref[0:CONV_HALO] = jnp.zeros((CONV_HALO, SUBLANES, LANES), F32)

    x = x_ref[...]
    shift = mod_ref[0, 0:1, :]
    scale = mod_ref[0, 1:2, :]
    h = (_rms(x) * gpre_ref[...]) * (1.0 + scale) + shift
    hb = h.astype(BF16)

    zg = _dot(hb, win_ref[:, _C_GLU:_C_GATE])
    u = zg[:, :CONV_DIM] * jax.nn.sigmoid(zg[:, CONV_DIM:])
    buf_ref[CONV_HALO:] = u.reshape(tm, SUBLANES, LANES)
    first = CONV_HALO - (CONV_KERNEL - 1)

    def conv_chunk(r0):
        acc = jnp.broadcast_to(cb_ref[...][None], (rows, SUBLANES, LANES))
        for k in range(CONV_KERNEL):
            acc = acc + cw_ref[k][None] * buf_ref[r0 + first + k:r0 + first + k + rows]
        acc = acc.reshape(rows, CONV_DIM)
        mu = jnp.mean(acc, axis=-1, keepdims=True)
        cen = acc - mu
        var = jnp.mean(cen * cen, axis=-1, keepdims=True)
        y = cen * lax.rsqrt(var + EPS) * lng_ref[...] + lnb_ref[...]
        y = y * jax.nn.sigmoid(y)
        act_ref[r0:r0 + rows, :] = y.astype(BF16)

    chunk_starts = list(range(0, tm, rows))
    per_stage = -(-len(chunk_starts) // 4)

    def conv_stage(i):
        for r0 in chunk_starts[i * per_stage:(i + 1) * per_stage]:
            conv_chunk(r0)

    gate_ref[...] = jax.nn.sigmoid(_dot(hb, win_ref[:, _C_GATE:_C_QLAT])).astype(BF16)
    conv_stage(0)

    ang_t = invf_ref[...] * pos_ref[0]
    reps = LANES // (QK_ROPE_DIM // 2)
    cos = jnp.concatenate([jnp.cos(ang_t)] * reps, axis=0).T
    sin = jnp.concatenate([jnp.sin(ang_t)] * reps, axis=0).T
    lane = lax.broadcasted_iota(jnp.int32, cos.shape, 1)
    first_half = (lane % QK_ROPE_DIM) < (QK_ROPE_DIM // 2)
    low = lane < QK_ROPE_DIM

    ql = _dot(hb, win_ref[:, _C_QLAT:_C_KVLAT])
    qn = (_rms(ql) * qg_ref[...]).astype(BF16)
    q = _dot(qn, wuq_ref[...]) * q_scale
    nope_w = N_HEADS * QK_NOPE_DIM
    for hp in range(N_HEADS // 2):
        qr = _rope_pairs(q[:, nope_w + LANES * hp: nope_w + LANES * (hp + 1)], cos, sin, first_half)
        qr_hi = pltpu.roll(qr, QK_ROPE_DIM, 1)
        for j, part in ((0, qr), (1, qr_hi)):
            hd = 2 * hp + j
            q_ref[0, hd, :, 0:LANES] = q[:, LANES * hd: LANES * (hd + 1)].astype(BF16)
            q_ref[0, hd, :, LANES:QK_PAD_DIM] = jnp.where(low, part, 0.0).astype(BF16)
    conv_stage(1)

    kvl = _dot(hb, win_ref[:, _C_KVLAT:_C_KR])
    kvn = (_rms(kvl) * kvg_ref[...]).astype(BF16)
    kn = _dot(kvn, wuk_ref[...])
    kr = _rope_pairs(_dot(hb, win_ref[:, _C_KR:_C_END]), cos, sin, first_half)
    kr = jnp.where(low, kr, 0.0).astype(BF16)
    for hd in range(N_HEADS):
        k_ref[0, hd, :, 0:LANES] = kn[:, LANES * hd: LANES * (hd + 1)].astype(BF16)
        k_ref[0, hd, :, LANES:QK_PAD_DIM] = kr
    conv_stage(2)

    vt = lax.dot_general(wuvt_ref[...], kvn, (((1,), (1,)), ((), ())), preferred_element_type=F32)
    for hd in range(N_HEADS):
        vt_ref[0, hd] = vt[V_HEAD_DIM * hd: V_HEAD_DIM * (hd + 1), :].astype(BF16)
    conv_stage(3)

    ya_ref[...] = _dot(act_ref[...], wco_ref[...]).astype(BF16)
    buf_ref[0:CONV_HALO] = buf_ref[tm:tm + CONV_HALO]


def _in_conv(x2, mod3, g_pre, w_in_r, q_norm_g, w_uq_r, kv_norm_g, w_uk, w_uvt, pos_f, inv_freq,
             conv_w8, conv_b8, ln_g, ln_b, w_conv_out, batch, seq, tm, rows):
    nt = seq // tm
    q_scale = (1.0 / math.sqrt(QK_NOPE_DIM + QK_ROPE_DIM)) * math.log2(math.e)
    tok = lambda i: (i, 0)
    heads = lambda i: (i // nt, 0, i % nt, 0)
    return pl.pallas_call(
        functools.partial(_in_conv_kernel, q_scale=q_scale, tm=tm, nt=nt, rows=rows),
        grid=(batch * nt,),
        in_specs=[
            pl.BlockSpec((tm, D_MODEL), tok),
            pl.BlockSpec((1, 6, D_MODEL), lambda i: (i // nt, 0, 0)),
            _const_spec((1, D_MODEL)),
            _const_spec(w_in_r.shape),
            _const_spec((1, Q_LORA_RANK)),
            _const_spec(w_uq_r.shape),
            _const_spec((1, KV_LORA_RANK)),
            _const_spec(w_uk.shape),
            _const_spec(w_uvt.shape),
            pl.BlockSpec((1, 1, tm), lambda i: (i, 0, 0)),
            _const_spec((QK_ROPE_DIM // 2, 1)),
            _const_spec(conv_w8.shape),
            _const_spec((SUBLANES, LANES)),
            _const_spec((1, CONV_DIM)),
            _const_spec((1, CONV_DIM)),
            _const_spec(w_conv_out.shape),
        ],
        out_specs=[
            pl.BlockSpec((tm, D_MODEL), tok),
            pl.BlockSpec((tm, 2 * D_MODEL), tok),
            pl.BlockSpec((1, N_HEADS, tm, QK_PAD_DIM), heads),
            pl.BlockSpec((1, N_HEADS, tm, QK_PAD_DIM), heads),
            pl.BlockSpec((1, N_HEADS, V_HEAD_DIM, tm), lambda i: (i // nt, 0, 0, i % nt)),
        ],
        out_shape=[
            jax.ShapeDtypeStruct((batch * seq, D_MODEL), BF16),
            jax.ShapeDtypeStruct((batch * seq, 2 * D_MODEL), BF16),
            jax.ShapeDtypeStruct((batch, N_HEADS, seq, QK_PAD_DIM), BF16),
            jax.ShapeDtypeStruct((batch, N_HEADS, seq, QK_PAD_DIM), BF16),
            jax.ShapeDtypeStruct((batch, N_HEADS, V_HEAD_DIM, seq), BF16),
        ],
        scratch_shapes=[pltpu.VMEM((CONV_HALO + tm, SUBLANES, LANES), F32),
                        pltpu.VMEM((tm, CONV_DIM), BF16)],
        compiler_params=_params("arbitrary"),
        name="in_conv",
    )(x2, mod3, g_pre, w_in_r, q_norm_g, w_uq_r, kv_norm_g, w_uk, w_uvt,
      pos_f.reshape(batch * nt, 1, tm), inv_freq,
      conv_w8, conv_b8, ln_g, ln_b, w_conv_out)


def _attn_kernel(q_ref, k_ref, vt_ref, *rest, tq, tk_full, tk_diag, hpb, n_weights):
    w_refs, o_ref, wb_refs = rest[:n_weights], rest[n_weights], rest[n_weights + 1:2 * n_weights + 1]
    m_ref, l_ref, acc_ref, s_ref = rest[2 * n_weights + 1:]
    for w_ref, wb_ref in zip(w_refs, wb_refs):
        wb_ref[...] = w_ref[...].astype(BF16)
    qi = pl.program_id(2)
    n_full = qi * (tq // tk_full)
    n_diag = tq // tk_diag
    diag_start = qi * tq
    m_ref[...] = jnp.full(m_ref.shape, -1e30, F32)
    l_ref[...] = jnp.zeros(l_ref.shape, F32)
    acc_ref[...] = jnp.zeros(acc_ref.shape, F32)

    def scores(hd, start, tk, q0=0):
        k = k_ref[0, hd, pl.ds(start, tk), :]
        s_ref[hd, :tk, q0:] = lax.dot_general(k, q_ref[0, hd, q0:, :], (((1,), (1,)), ((), ())),
                                              preferred_element_type=F32)

    def update(hd, start, tk, q0=0, diag=False):
        s = s_ref[hd, :tk, q0:]
        if diag:
            kpos = lax.broadcasted_iota(jnp.int32, (tk, tk), 0)
            qpos = lax.broadcasted_iota(jnp.int32, (tk, tk), 1)
            tri = jnp.where(qpos >= kpos, s[:, :tk], -1e30)
            s = tri if q0 + tk == tq else jnp.concatenate([tri, s[:, tk:]], axis=1)
        m_old = m_ref[hd, :, q0:]
        m_new = jnp.maximum(m_old, jnp.max(s, axis=0, keepdims=True))
        alpha = jnp.exp2(m_old - m_new)
        p = jnp.exp2(s - m_new)
        m_ref[hd, :, q0:] = m_new
        l_ref[hd, :, q0:] = alpha * l_ref[hd, :, q0:] + jnp.sum(p, axis=0, keepdims=True)
        vt = vt_ref[0, hd, :, pl.ds(start, tk)]
        acc_ref[hd, :, q0:] = alpha * acc_ref[hd, :, q0:] + _dot(vt, p.astype(BF16))

    def full_start(j):
        return pl.multiple_of(j * tk_full, tk_full)

    def diag_tile(d):
        return pl.multiple_of(diag_start + d * tk_diag, tk_diag), tk_diag, d * tk_diag

    if tk_full == tk_diag:
        scores(0, full_start(0), tk_full)
    else:
        pl.when(n_full > 0)(lambda: scores(0, full_start(0), tk_full))
        pl.when(n_full == 0)(lambda: scores(0, *diag_tile(0)))

    def full_round(j, next_scores):
        for hd in range(hpb):
            if hd + 1 < hpb:
                scores(hd + 1, full_start(j), tk_full)
            else:
                next_scores()
            update(hd, full_start(j), tk_full)

    def body(j, carry):
        full_round(j, lambda: scores(0, full_start(j + 1), tk_full))
        return carry

    lax.fori_loop(0, jnp.maximum(n_full - 1, 0), body, 0)
    pl.when(n_full > 0)(lambda: full_round(n_full - 1, lambda: scores(0, *diag_tile(0))))
    for d in range(n_diag):
        for hd in range(hpb):
            if hd + 1 < hpb:
                scores(hd + 1, *diag_tile(d))
            elif d + 1 < n_diag:
                scores(0, *diag_tile(d + 1))
            update(hd, *diag_tile(d), True)
    for hd in range(hpb):
        o = acc_ref[hd] / l_ref[hd]
        o_ref[0, :, V_HEAD_DIM * hd: V_HEAD_DIM * (hd + 1)] = o.T.astype(BF16)


def _attn(q4, k4, vt4, later_weights, tq, tk_full, tk_diag, hpb):
    batch, heads, seq, _ = q4.shape
    grid = (batch, heads // hpb, seq // tq)
    steps = grid[0] * grid[1] * grid[2]
    slab = lambda b, h, i: ((b * grid[1] + h) * grid[2] + i, 0)
    w_specs = [pl.BlockSpec((w.shape[0] // steps, w.shape[1]), slab) for w in later_weights]
    outs = pl.pallas_call(
        functools.partial(_attn_kernel, tq=tq, tk_full=tk_full, tk_diag=tk_diag, hpb=hpb,
                          n_weights=len(later_weights)),
        grid=grid,
        in_specs=[
            pl.BlockSpec((1, hpb, tq, QK_PAD_DIM), lambda b, h, i: (b, h, i, 0)),
            pl.BlockSpec((1, hpb, seq, QK_PAD_DIM), lambda b, h, i: (b, h, 0, 0)),
            pl.BlockSpec((1, hpb, V_HEAD_DIM, seq), lambda b, h, i: (b, h, 0, 0)),
        ] + w_specs,
        out_specs=[pl.BlockSpec((1, tq, hpb * V_HEAD_DIM), lambda b, h, i: (b, i, h))] + w_specs,
        out_shape=[jax.ShapeDtypeStruct((batch, seq, heads * V_HEAD_DIM), BF16)]
        + [jax.ShapeDtypeStruct(w.shape, BF16) for w in later_weights],
        scratch_shapes=[pltpu.VMEM((hpb, 1, tq), F32), pltpu.VMEM((hpb, 1, tq), F32),
                        pltpu.VMEM((hpb, V_HEAD_DIM, tq), F32),
                        pltpu.VMEM((hpb, max(tk_full, tk_diag), tq), F32)],
        compiler_params=_params("arbitrary", "arbitrary", "arbitrary"),
        name="attn",
    )(q4, k4, vt4, *later_weights)
    return outs[0], outs[1:]


def _mix_out_kernel(o_ref, ya_ref, gate_ref, x_ref, mod_ref, gpost_ref, wao_ref, wout_ref, x1_ref):
    y_b = _dot(o_ref[...], wao_ref[...])
    g_a = gate_ref[:, :D_MODEL].astype(F32)
    g_b = gate_ref[:, D_MODEL:].astype(F32)
    merged = g_a * ya_ref[...].astype(F32) + g_b * y_b
    y = _dot(merged.astype(BF16), wout_ref[...])
    gate1 = mod_ref[0, 2:3, :]
    x1_ref[...] = x_ref[...] + gate1 * (_rms(y) * gpost_ref[...])


def _mix_out(o2, ya2, gates, x2, mod3, g_post, w_attn_out, w_out, seq, tm):
    tokens = x2.shape[0]
    nt = seq // tm
    tok = lambda i: (i, 0)
    return pl.pallas_call(
        _mix_out_kernel,
        grid=(tokens // tm,),
        in_specs=[
            pl.BlockSpec((tm, D_MODEL), tok),
            pl.BlockSpec((tm, D_MODEL), tok),
            pl.BlockSpec((tm, 2 * D_MODEL), tok),
            pl.BlockSpec((tm, D_MODEL), tok),
            pl.BlockSpec((1, 6, D_MODEL), lambda i: (i // nt, 0, 0)),
            _const_spec((1, D_MODEL)),
            _const_spec(w_attn_out.shape),
            _const_spec(w_out.shape),
        ],
        out_specs=pl.BlockSpec((tm, D_MODEL), tok),
        out_shape=jax.ShapeDtypeStruct((tokens, D_MODEL), F32),
        compiler_params=_params("arbitrary"),
        name="mix_out",
    )(o2, ya2, gates, x2, mod3, g_post, w_attn_out, w_out)


def _mlp_kernel(x_ref, mod_ref, gpre_ref, gpost_ref, w1_ref, w2_ref, o_ref):
    x = x_ref[...]
    shift = mod_ref[0, 3:4, :]
    scale = mod_ref[0, 4:5, :]
    gate = mod_ref[0, 5:6, :]
    h = (_rms(x) * gpre_ref[...]) * (1.0 + scale) + shift
    hid = jnp.maximum(_dot(h.astype(BF16), w1_ref[...]), 0.0)
    y = _dot((hid * hid).astype(BF16), w2_ref[...])
    o_ref[...] = x + gate * (_rms(y) * gpost_ref[...])


def _mlp(x1, mod3, g_pre, g_post, w1, w2, seq, tm):
    tokens = x1.shape[0]
    nt = seq // tm
    tok = lambda i: (i, 0)
    return pl.pallas_call(
        _mlp_kernel,
        grid=(tokens // tm,),
        in_specs=[
            pl.BlockSpec((tm, D_MODEL), tok),
            pl.BlockSpec((1, 6, D_MODEL), lambda i: (i // nt, 0, 0)),
            _const_spec((1, D_MODEL)),
            _const_spec((1, D_MODEL)),
            _const_spec(w1.shape),
            _const_spec(w2.shape),
        ],
        out_specs=pl.BlockSpec((tm, D_MODEL), tok),
        out_shape=jax.ShapeDtypeStruct((tokens, D_MODEL), F32),
        compiler_params=_params("arbitrary"),
        name="mlp",
    )(x1, mod3, g_pre, g_post, w1, w2)


def _head_major_cols(w, split):
    k = w.shape[0]
    w3 = w.reshape(k, N_HEADS, -1)
    return jnp.concatenate([w3[:, :, :split].reshape(k, -1), w3[:, :, split:].reshape(k, -1)], axis=1)


def kernel(x, c, positions, w_ada, b_ada, g_pre_mix, g_post_mix, g_pre_mlp, g_post_mlp, w_in, conv_w, conv_b, conv_norm_g, conv_norm_b, w_conv_out, q_norm_g, w_uq, kv_norm_g, w_ukv, w_attn_out, w_out, w_mlp_in, w_mlp_out):
    batch, seq, d = x.shape
    depth = w_ada.shape[0]
    tokens = batch * seq
    row = lambda v: v.reshape(1, -1)

    inv_freq = 1.0 / (ROPE_THETA ** (jnp.arange(0, QK_ROPE_DIM, 2, dtype=F32) / QK_ROPE_DIM))
    inv_freq = inv_freq.reshape(-1, 1)
    pos_f = positions.astype(F32)
    c_pad = jnp.pad(c, ((0, 8 - batch), (0, 0)))

    x2 = x.reshape(tokens, d)
    for l in range(depth):
        w_in_r = _prep_w_in(w_in[l])
        w_uq_r = _head_major_cols(w_uq[l], QK_NOPE_DIM).astype(BF16)
        w_ukv_r = _head_major_cols(w_ukv[l], QK_NOPE_DIM).astype(BF16)
        w_uk = w_ukv_r[:, :N_HEADS * QK_NOPE_DIM]
        w_uvt = w_ukv_r[:, N_HEADS * QK_NOPE_DIM:].T

        mod = _ada(c_pad, w_ada[l], row(b_ada[l]))
        mod3 = mod[:batch].reshape(batch, 6, d)

        conv_w8 = conv_w[l].reshape(CONV_KERNEL, SUBLANES, LANES)
        conv_b8 = conv_b[l].reshape(SUBLANES, LANES)
        y_a, gates, q4, k4, vt4 = _in_conv(x2, mod3, row(g_pre_mix[l]), w_in_r, row(q_norm_g[l]), w_uq_r,
                                           row(kv_norm_g[l]), w_uk, w_uvt, pos_f, inv_freq, conv_w8, conv_b8,
                                           row(conv_norm_g[l]), row(conv_norm_b[l]), w_conv_out[l].astype(BF16),
                                           batch, seq, tm=512, rows=32)
        o, (w_ao, w_o, w_1, w_2) = _attn(q4, k4, vt4, (w_attn_out[l], w_out[l], w_mlp_in[l], w_mlp_out[l]),
                                         tq=1024, tk_full=512, tk_diag=512, hpb=4)
        x2 = _mix_out(o.reshape(tokens, d), y_a, gates, x2, mod3, row(g_post_mix[l]), w_ao, w_o, seq, tm=512)
        x2 = _mlp(x2, mod3, row(g_pre_mlp[l]), row(g_post_mlp[l]), w_1, w_2, seq, tm=512)
    return x2.reshape(batch, seq, d)
```

```python
import functools
import math

import jax
import jax.numpy as jnp
from jax import lax
from jax.experimental import pallas as pl
from jax.experimental.pallas import tpu as pltpu

D_MODEL = 1024
CONV_DIM = 1024
CONV_KERNEL = 31
N_HEADS = 8
QK_NOPE_DIM = 128
QK_ROPE_DIM = 64
V_HEAD_DIM = 128
Q_LORA_RANK = 384
KV_LORA_RANK = 256
ROPE_THETA = 10000.0
D_FF = 4 * D_MODEL
EPS = 1e-6

LANES = 128
SUBLANES = 8
QK_PAD_DIM = 2 * LANES
CONV_HALO = 32
VMEM_LIMIT = 56 * 1024 * 1024

_C_GLU = 0
_C_GATE = 2 * CONV_DIM
_C_QLAT = _C_GATE + 2 * D_MODEL
_C_KVLAT = _C_QLAT + Q_LORA_RANK
_C_KR = _C_KVLAT + KV_LORA_RANK
_C_END = _C_KR + LANES

BF16 = jnp.bfloat16
F32 = jnp.float32


def _dot(a, b):
    return jnp.dot(a, b, preferred_element_type=F32)


def _rms(v):
    return v * lax.rsqrt(jnp.mean(v * v, axis=-1, keepdims=True) + EPS)


def _params(*sem):
    return pltpu.CompilerParams(dimension_semantics=sem, vmem_limit_bytes=VMEM_LIMIT)


def _const_spec(shape):
    nd = len(shape)
    return pl.BlockSpec(shape, lambda *_: (0,) * nd, pipeline_mode=pl.Buffered(1))


def _ada_kernel(c_ref, w_ref, b_ref, o_ref):
    c = c_ref[...]
    act = c * jax.nn.sigmoid(c)
    o_ref[...] = _dot(act.astype(BF16), w_ref[...].astype(BF16)) + b_ref[...]


def _ada(c_pad, w_ada, b_ada):
    rows, d = c_pad.shape
    n = w_ada.shape[1]
    tn = 1024
    return pl.pallas_call(
        _ada_kernel,
        grid=(n // tn,),
        in_specs=[
            pl.BlockSpec((rows, d), lambda j: (0, 0)),
            pl.BlockSpec((d, tn), lambda j: (0, j)),
            pl.BlockSpec((1, tn), lambda j: (0, j)),
        ],
        out_specs=pl.BlockSpec((rows, tn), lambda j: (0, j)),
        out_shape=jax.ShapeDtypeStruct((rows, n), F32),
        compiler_params=_params("arbitrary"),
        name="ada",
    )(c_pad, w_ada, b_ada)


_S_QLAT = 2 * CONV_DIM
_S_KVLAT = _S_QLAT + Q_LORA_RANK
_S_KR = _S_KVLAT + KV_LORA_RANK
_S_GATE = _S_KR + QK_ROPE_DIM


def _prep_w_in_kernel(wt_ref, o_ref):
    def move(dst, src, n):
        for off in range(0, n, QK_PAD_DIM):
            m = min(QK_PAD_DIM, n - off)
            o_ref[:, dst + off:dst + off + m] = wt_ref[src + off:src + off + m, :].T.astype(BF16)

    move(_C_GLU, 0, _S_QLAT)
    move(_C_GATE, _S_GATE, _C_QLAT - _C_GATE)
    move(_C_QLAT, _S_QLAT, Q_LORA_RANK)
    move(_C_KVLAT, _S_KVLAT, KV_LORA_RANK)
    kr = wt_ref[_S_KR:_S_KR + LANES, :].T
    lane = lax.broadcasted_iota(jnp.int32, kr.shape, 1)
    o_ref[:, _C_KR:_C_END] = jnp.where(lane < QK_ROPE_DIM, kr, pltpu.roll(kr, QK_ROPE_DIM, 1)).astype(BF16)


def _prep_w_in(w):
    k, n = w.shape
    return pl.pallas_call(
        _prep_w_in_kernel,
        grid=(1,),
        in_specs=[_const_spec((n, k))],
        out_specs=pl.BlockSpec((k, _C_END), lambda i: (0, 0)),
        out_shape=jax.ShapeDtypeStruct((k, _C_END), BF16),
        compiler_params=_params("arbitrary"),
        name="prep_w_in",
    )(w.T)


def _rope_pairs(t, cos, sin, first_half):
    rot = jnp.where(first_half, -pltpu.roll(t, LANES - QK_ROPE_DIM // 2, 1),
                    pltpu.roll(t, QK_ROPE_DIM // 2, 1))
    return t * cos + rot * sin


def _in_conv_kernel(x_ref, mod_ref, gpre_ref, win_ref, qg_ref, wuq_ref, kvg_ref, wuk_ref, wuvt_ref,
                    pos_ref, invf_ref, cw_ref, cb_ref, lng_ref, lnb_ref, wco_ref,
                    ya_ref, gate_ref, q_ref, k_ref, vt_ref, buf_ref, act_ref, *, q_scale, tm, nt, rows):
    t = pl.program_id(0) % nt

    @pl.when(t == 0)
    def _():
        buf_ref[0:CONV_HALO] = jnp.zeros((CONV_HALO, SUBLANES, LANES), F32)

    x = x_ref[...]
    shift = mod_ref[0, 0:1, :]
    scale = mod_ref[0, 1:2, :]
    h = (_rms(x) * gpre_ref[...]) * (1.0 + scale) + shift
    hb = h.astype(BF16)

    zg = _dot(hb, win_ref[:, _C_GLU:_C_GATE])
    u = zg[:, :CONV_DIM] * jax.nn.sigmoid(zg[:, CONV_DIM:])
    buf_ref[CONV_HALO:] = u.reshape(tm, SUBLANES, LANES)
    first = CONV_HALO - (CONV_KERNEL - 1)

    def conv_chunk(r0):
        acc = jnp.broadcast_to(cb_ref[...][None], (rows, SUBLANES, LANES))
        for k in range(CONV_KERNEL):
            acc = acc + cw_ref[k][None] * buf_ref[r0 + first + k:r0 + first + k + rows]
        acc = acc.reshape(rows, CONV_DIM)
        mu = jnp.mean(acc, axis=-1, keepdims=True)
        cen = acc - mu
        var = jnp.mean(cen * cen, axis=-1, keepdims=True)
        y = cen * lax.rsqrt(var + EPS) * lng_ref[...] + lnb_ref[...]
        y = y * jax.nn.sigmoid(y)
        act_ref[r0:r0 + rows, :] = y.astype(BF16)

    chunk_starts = list(range(0, tm, rows))
    per_stage = -(-len(chunk_starts) // 4)

    def conv_stage(i):
        for r0 in chunk_starts[i * per_stage:(i + 1) * per_stage]:
            conv_chunk(r0)

    gate_ref[...] = jax.nn.sigmoid(_dot(hb, win_ref[:, _C_GATE:_C_QLAT])).astype(BF16)
    conv_stage(0)

    ang_t = invf_ref[...] * pos_ref[0]
    reps = LANES // (QK_ROPE_DIM // 2)
    cos = jnp.concatenate([jnp.cos(ang_t)] * reps, axis=0).T
    sin = jnp.concatenate([jnp.sin(ang_t)] * reps, axis=0).T
    lane = lax.broadcasted_iota(jnp.int32, cos.shape, 1)
    first_half = (lane % QK_ROPE_DIM) < (QK_ROPE_DIM // 2)
    low = lane < QK_ROPE_DIM

    ql = _dot(hb, win_ref[:, _C_QLAT:_C_KVLAT])
    qn = (_rms(ql) * qg_ref[...]).astype(BF16)
    q = _dot(qn, wuq_ref[...]) * q_scale
    nope_w = N_HEADS * QK_NOPE_DIM
    for hp in range(N_HEADS // 2):
        qr = _rope_pairs(q[:, nope_w + LANES * hp: nope_w + LANES * (hp + 1)], cos, sin, first_half)
        qr_hi = pltpu.roll(qr, QK_ROPE_DIM, 1)
        for j, part in ((0, qr), (1, qr_hi)):
            hd = 2 * hp + j
            q_ref[0, hd, :, 0:LANES] = q[:, LANES * hd: LANES * (hd + 1)].astype(BF16)
            q_ref[0, hd, :, LANES:QK_PAD_DIM] = jnp.where(low, part, 0.0).astype(BF16)
    conv_stage(1)

    kvl = _dot(hb, win_ref[:, _C_KVLAT:_C_KR])
    kvn = (_rms(kvl) * kvg_ref[...]).astype(BF16)
    kn = _dot(kvn, wuk_ref[...])
    kr = _rope_pairs(_dot(hb, win_ref[:, _C_KR:_C_END]), cos, sin, first_half)
    kr = jnp.where(low, kr, 0.0).astype(BF16)
    for hd in range(N_HEADS):
        k_ref[0, hd, :, 0:LANES] = kn[:, LANES * hd: LANES * (hd + 1)].astype(BF16)
        k_ref[0, hd, :, LANES:QK_PAD_DIM] = kr
    conv_stage(2)

    vt = lax.dot_general(wuvt_ref[...], kvn, (((1,), (1,)), ((), ())), preferred_element_type=F32)
    for hd in range(N_HEADS):
        vt_ref[0, hd] = vt[V_HEAD_DIM * hd: V_HEAD_DIM * (hd + 1), :].astype(BF16)
    conv_stage(3)

    ya_ref[...] = _dot(act_ref[...], wco_ref[...]).astype(BF16)
    buf_ref[0:CONV_HALO] = buf_ref[tm:tm + CONV_HALO]


def _in_conv(x2, mod3, g_pre, w_in_r, q_norm_g, w_uq_r, kv_norm_g, w_uk, w_uvt, pos_f, inv_freq,
             conv_w8, conv_b8, ln_g, ln_b, w_conv_out, batch, seq, tm, rows):
    nt = seq // tm
    q_scale = (1.0 / math.sqrt(QK_NOPE_DIM + QK_ROPE_DIM)) * math.log2(math.e)
    tok = lambda i: (i, 0)
    heads = lambda i: (i // nt, 0, i % nt, 0)
    return pl.pallas_call(
        functools.partial(_in_conv_kernel, q_scale=q_scale, tm=tm, nt=nt, rows=rows),
        grid=(batch * nt,),
        in_specs=[
            pl.BlockSpec((tm, D_MODEL), tok),
            pl.BlockSpec((1, 6, D_MODEL), lambda i: (i // nt, 0, 0)),
            _const_spec((1, D_MODEL)),
            _const_spec(w_in_r.shape),
            _const_spec((1, Q_LORA_RANK)),
            _const_spec(w_uq_r.shape),
            _const_spec((1, KV_LORA_RANK)),
            _const_spec(w_uk.shape),
            _const_spec(w_uvt.shape),
            pl.BlockSpec((1, 1, tm), lambda i: (i, 0, 0)),
            _const_spec((QK_ROPE_DIM // 2, 1)),
            _const_spec(conv_w8.shape),
            _const_spec((SUBLANES, LANES)),
            _const_spec((1, CONV_DIM)),
            _const_spec((1, CONV_DIM)),
            _const_spec(w_conv_out.shape),
        ],
        out_specs=[
            pl.BlockSpec((tm, D_MODEL), tok),
            pl.BlockSpec((tm, 2 * D_MODEL), tok),
            pl.BlockSpec((1, N_HEADS, tm, QK_PAD_DIM), heads),
            pl.BlockSpec((1, N_HEADS, tm, QK_PAD_DIM), heads),
            pl.BlockSpec((1, N_HEADS, V_HEAD_DIM, tm), lambda i: (i // nt, 0, 0, i % nt)),
        ],
        out_shape=[
            jax.ShapeDtypeStruct((batch * seq, D_MODEL), BF16),
            jax.ShapeDtypeStruct((batch * seq, 2 * D_MODEL), BF16),
            jax.ShapeDtypeStruct((batch, N_HEADS, seq, QK_PAD_DIM), BF16),
            jax.ShapeDtypeStruct((batch, N_HEADS, seq, QK_PAD_DIM), BF16),
            jax.ShapeDtypeStruct((batch, N_HEADS, V_HEAD_DIM, seq), BF16),
        ],
        scratch_shapes=[pltpu.VMEM((CONV_HALO + tm, SUBLANES, LANES), F32),
                        pltpu.VMEM((tm, CONV_DIM), BF16)],
        compiler_params=_params("arbitrary"),
        name="in_conv",
    )(x2, mod3, g_pre, w_in_r, q_norm_g, w_uq_r, kv_norm_g, w_uk, w_uvt,
      pos_f.reshape(batch * nt, 1, tm), inv_freq,
      conv_w8, conv_b8, ln_g, ln_b, w_conv_out)


def _attn_kernel(q_ref, k_ref, vt_ref, *rest, tq, tk_full, tk_diag, hpb, n_weights):
    w_refs, o_ref, wb_refs = rest[:n_weights], rest[n_weights], rest[n_weights + 1:2 * n_weights + 1]
    m_ref, l_ref, acc_ref, s_ref = rest[2 * n_weights + 1:]
    for w_ref, wb_ref in zip(w_refs, wb_refs):
        wb_ref[...] = w_ref[...].astype(BF16)
    qi = pl.program_id(2)
    n_full = qi * (tq // tk_full)
    n_diag = tq // tk_diag
    diag_start = qi * tq
    m_ref[...] = jnp.full(m_ref.shape, -1e30, F32)
    l_ref[...] = jnp.zeros(l_ref.shape, F32)
    acc_ref[...] = jnp.zeros(acc_ref.shape, F32)

    def scores(hd, start, tk, q0=0):
        k = k_ref[0, hd, pl.ds(start, tk), :]
        s_ref[hd, :tk, q0:] = lax.dot_general(k, q_ref[0, hd, q0:, :], (((1,), (1,)), ((), ())),
                                              preferred_element_type=F32)

    def update(hd, start, tk, q0=0, diag=False):
        s = s_ref[hd, :tk, q0:]
        if diag:
            kpos = lax.broadcasted_iota(jnp.int32, (tk, tk), 0)
            qpos = lax.broadcasted_iota(jnp.int32, (tk, tk), 1)
            tri = jnp.where(qpos >= kpos, s[:, :tk], -1e30)
            s = tri if q0 + tk == tq else jnp.concatenate([tri, s[:, tk:]], axis=1)
        m_old = m_ref[hd, :, q0:]
        m_new = jnp.maximum(m_old, jnp.max(s, axis=0, keepdims=True))
        alpha = jnp.exp2(m_old - m_new)
        p = jnp.exp2(s - m_new)
        m_ref[hd, :, q0:] = m_new
        l_ref[hd, :, q0:] = alpha * l_ref[hd, :, q0:] + jnp.sum(p, axis=0, keepdims=True)
        vt = vt_ref[0, hd, :, pl.ds(start, tk)]
        acc_ref[hd, :, q0:] = alpha * acc_ref[hd, :, q0:] + _dot(vt, p.astype(BF16))

    def full_start(j):
        return pl.multiple_of(j * tk_full, tk_full)

    def diag_tile(d):
        return pl.multiple_of(diag_start + d * tk_diag, tk_diag), tk_diag, d * tk_diag

    if tk_full == tk_diag:
        scores(0, full_start(0), tk_full)
    else:
        pl.when(n_full > 0)(lambda: scores(0, full_start(0), tk_full))
        pl.when(n_full == 0)(lambda: scores(0, *diag_tile(0)))

    def full_round(j, next_scores):
        for hd in range(hpb):
            if hd + 1 < hpb:
                scores(hd + 1, full_start(j), tk_full)
            else:
                next_scores()
            update(hd, full_start(j), tk_full)

    def body(j, carry):
        full_round(j, lambda: scores(0, full_start(j + 1), tk_full))
        return carry

    lax.fori_loop(0, jnp.maximum(n_full - 1, 0), body, 0)
    pl.when(n_full > 0)(lambda: full_round(n_full - 1, lambda: scores(0, *diag_tile(0))))
    for d in range(n_diag):
        for hd in range(hpb):
            if hd + 1 < hpb:
                scores(hd + 1, *diag_tile(d))
            elif d + 1 < n_diag:
                scores(0, *diag_tile(d + 1))
            update(hd, *diag_tile(d), True)
    for hd in range(hpb):
        o = acc_ref[hd] / l_ref[hd]
        o_ref[0, :, V_HEAD_DIM * hd: V_HEAD_DIM * (hd + 1)] = o.T.astype(BF16)


def _attn(q4, k4, vt4, later_weights, tq, tk_full, tk_diag, hpb):
    batch, heads, seq, _ = q4.shape
    grid = (batch, heads // hpb, seq // tq)
    steps = grid[0] * grid[1] * grid[2]
    slab = lambda b, h, i: ((b * grid[1] + h) * grid[2] + i, 0)
    w_specs = [pl.BlockSpec((w.shape[0] // steps, w.shape[1]), slab) for w in later_weights]
    outs = pl.pallas_call(
        functools.partial(_attn_kernel, tq=tq, tk_full=tk_full, tk_diag=tk_diag, hpb=hpb,
                          n_weights=len(later_weights)),
        grid=grid,
        in_specs=[
            pl.BlockSpec((1, hpb, tq, QK_PAD_DIM), lambda b, h, i: (b, h, i, 0)),
            pl.BlockSpec((1, hpb, seq, QK_PAD_DIM), lambda b, h, i: (b, h, 0, 0)),
            pl.BlockSpec((1, hpb, V_HEAD_DIM, seq), lambda b, h, i: (b, h, 0, 0)),
        ] + w_specs,
        out_specs=[pl.BlockSpec((1, tq, hpb * V_HEAD_DIM), lambda b, h, i: (b, i, h))] + w_specs,
        out_shape=[jax.ShapeDtypeStruct((batch, seq, heads * V_HEAD_DIM), BF16)]
        + [jax.ShapeDtypeStruct(w.shape, BF16) for w in later_weights],
        scratch_shapes=[pltpu.VMEM((hpb, 1, tq), F32), pltpu.VMEM((hpb, 1, tq), F32),
                        pltpu.VMEM((hpb, V_HEAD_DIM, tq), F32),
                        pltpu.VMEM((hpb, max(tk_full, tk_diag), tq), F32)],
        compiler_params=_params("arbitrary", "arbitrary", "arbitrary"),
        name="attn",
    )(q4, k4, vt4, *later_weights)
    return outs[0], outs[1:]


def _mix_mlp_kernel(o_ref, ya_ref, gate_ref, x_ref, mod_ref, gpost_mix_ref, gpre_mlp_ref, gpost_mlp_ref,
                    wao_ref, wout_ref, w1_ref, w2_ref, out_ref):
    x = x_ref[...]
    y_b = _dot(o_ref[...], wao_ref[...])
    g_a = gate_ref[:, :D_MODEL].astype(F32)
    g_b = gate_ref[:, D_MODEL:].astype(F32)
    merged = g_a * ya_ref[...].astype(F32) + g_b * y_b
    y = _dot(merged.astype(BF16), wout_ref[...])
    x1 = x + mod_ref[0, 2:3, :] * (_rms(y) * gpost_mix_ref[...])

    shift = mod_ref[0, 3:4, :]
    scale = mod_ref[0, 4:5, :]
    h = (_rms(x1) * gpre_mlp_ref[...]) * (1.0 + scale) + shift
    hid = jnp.maximum(_dot(h.astype(BF16), w1_ref[...]), 0.0)
    y2 = _dot((hid * hid).astype(BF16), w2_ref[...])
    out_ref[...] = x1 + mod_ref[0, 5:6, :] * (_rms(y2) * gpost_mlp_ref[...])


def _mix_mlp(o2, ya2, gates, x2, mod3, g_post_mix, g_pre_mlp, g_post_mlp, w_attn_out, w_out, w1, w2, seq, tm):
    tokens = x2.shape[0]
    nt = seq // tm
    tok = lambda i: (i, 0)
    return pl.pallas_call(
        _mix_mlp_kernel,
        grid=(tokens // tm,),
        in_specs=[
            pl.BlockSpec((tm, D_MODEL), tok),
            pl.BlockSpec((tm, D_MODEL), tok),
            pl.BlockSpec((tm, 2 * D_MODEL), tok),
            pl.BlockSpec((tm, D_MODEL), tok),
            pl.BlockSpec((1, 6, D_MODEL), lambda i: (i // nt, 0, 0)),
            _const_spec((1, D_MODEL)),
            _const_spec((1, D_MODEL)),
            _const_spec((1, D_MODEL)),
            _const_spec(w_attn_out.shape),
            _const_spec(w_out.shape),
            _const_spec(w1.shape),
            _const_spec(w2.shape),
        ],
        out_specs=pl.BlockSpec((tm, D_MODEL), tok),
        out_shape=jax.ShapeDtypeStruct((tokens, D_MODEL), F32),
        compiler_params=_params("arbitrary"),
        name="mix_mlp",
    )(o2, ya2, gates, x2, mod3, g_post_mix, g_pre_mlp, g_post_mlp, w_attn_out, w_out, w1, w2)


def _head_major_cols(w, split):
    k = w.shape[0]
    w3 = w.reshape(k, N_HEADS, -1)
    return jnp.concatenate([w3[:, :, :split].reshape(k, -1), w3[:, :, split:].reshape(k, -1)], axis=1)


def kernel(x, c, positions, w_ada, b_ada, g_pre_mix, g_post_mix, g_pre_mlp, g_post_mlp, w_in, conv_w, conv_b, conv_norm_g, conv_norm_b, w_conv_out, q_norm_g, w_uq, kv_norm_g, w_ukv, w_attn_out, w_out, w_mlp_in, w_mlp_out):
    batch, seq, d = x.shape
    depth = w_ada.shape[0]
    tokens = batch * seq
    row = lambda v: v.reshape(1, -1)

    inv_freq = 1.0 / (ROPE_THETA ** (jnp.arange(0, QK_ROPE_DIM, 2, dtype=F32) / QK_ROPE_DIM))
    inv_freq = inv_freq.reshape(-1, 1)
    pos_f = positions.astype(F32)
    c_pad = jnp.pad(c, ((0, 8 - batch), (0, 0)))

    x2 = x.reshape(tokens, d)
    for l in range(depth):
        w_in_r = _prep_w_in(w_in[l])
        w_uq_r = _head_major_cols(w_uq[l], QK_NOPE_DIM).astype(BF16)
        w_ukv_r = _head_major_cols(w_ukv[l], QK_NOPE_DIM).astype(BF16)
        w_uk = w_ukv_r[:, :N_HEADS * QK_NOPE_DIM]
        w_uvt = w_ukv_r[:, N_HEADS * QK_NOPE_DIM:].T

        mod = _ada(c_pad, w_ada[l], row(b_ada[l]))
        mod3 = mod[:batch].reshape(batch, 6, d)

        conv_w8 = conv_w[l].reshape(CONV_KERNEL, SUBLANES, LANES)
        conv_b8 = conv_b[l].reshape(SUBLANES, LANES)
        y_a, gates, q4, k4, vt4 = _in_conv(x2, mod3, row(g_pre_mix[l]), w_in_r, row(q_norm_g[l]), w_uq_r,
                                           row(kv_norm_g[l]), w_uk, w_uvt, pos_f, inv_freq, conv_w8, conv_b8,
                                           row(conv_norm_g[l]), row(conv_norm_b[l]), w_conv_out[l].astype(BF16),
                                           batch, seq, tm=512, rows=32)
        o, (w_ao, w_o, w_1, w_2) = _attn(q4, k4, vt4, (w_attn_out[l], w_out[l], w_mlp_in[l], w_mlp_out[l]),
                                         tq=1024, tk_full=512, tk_diag=512, hpb=4)
        x2 = _mix_mlp(o.reshape(tokens, d), y_a, gates, x2, mod3, row(g_post_mix[l]), row(g_pre_mlp[l]),
                      row(g_post_mlp[l]), w_ao, w_o, w_1, w_2, seq, tm=512)
    return x2.reshape(batch, seq, d)
```

```python
import functools
import math

import jax
import jax.numpy as jnp
from jax import lax
from jax.experimental import pallas as pl
from jax.experimental.pallas import tpu as pltpu

D_MODEL = 1024
CONV_DIM = 1024
CONV_KERNEL = 31
N_HEADS = 8
QK_NOPE_DIM = 128
QK_ROPE_DIM = 64
V_HEAD_DIM = 128
Q_LORA_RANK = 384
KV_LORA_RANK = 256
ROPE_THETA = 10000.0
EPS = 1e-6

LANES = 128
SUBLANES = 8
QK_PAD_DIM = 2 * LANES
DENOM_ROWS = 16
CONV_HALO = 32
VMEM_LIMIT = 56 * 1024 * 1024

IN_CONV_ROWS = 512
CONV_CHUNK_ROWS = 32
ATTN_Q_ROWS = 1024
ATTN_K_ROWS = 512
ATTN_HEADS_PER_STEP = 4
MIX_MLP_ROWS = 512

_C_GLU = 0
_C_GATE = 2 * CONV_DIM
_C_QLAT = _C_GATE + 2 * D_MODEL
_C_KVLAT = _C_QLAT + Q_LORA_RANK
_C_KR = _C_KVLAT + KV_LORA_RANK
_C_END = _C_KR + LANES

BF16 = jnp.bfloat16
F32 = jnp.float32


def _dot(a, b):
    return jnp.dot(a, b, preferred_element_type=F32)


def _rms(v):
    return v * lax.rsqrt(jnp.mean(v * v, axis=-1, keepdims=True) + EPS)


def _params(*sem):
    return pltpu.CompilerParams(dimension_semantics=sem, vmem_limit_bytes=VMEM_LIMIT)


def _const_spec(shape):
    nd = len(shape)
    return pl.BlockSpec(shape, lambda *_: (0,) * nd, pipeline_mode=pl.Buffered(1))


def _ada_kernel(c_ref, w_ref, b_ref, o_ref):
    c = c_ref[...]
    act = c * jax.nn.sigmoid(c)
    o_ref[...] = _dot(act.astype(BF16), w_ref[...].astype(BF16)) + b_ref[...]


def _ada(c_pad, w_ada, b_ada):
    rows, d = c_pad.shape
    n = w_ada.shape[1]
    tn = 1024
    return pl.pallas_call(
        _ada_kernel,
        grid=(n // tn,),
        in_specs=[
            pl.BlockSpec((rows, d), lambda j: (0, 0)),
            pl.BlockSpec((d, tn), lambda j: (0, j)),
            pl.BlockSpec((1, tn), lambda j: (0, j)),
        ],
        out_specs=pl.BlockSpec((rows, tn), lambda j: (0, j)),
        out_shape=jax.ShapeDtypeStruct((rows, n), F32),
        compiler_params=_params("arbitrary"),
        name="ada",
    )(c_pad, w_ada, b_ada)


_S_QLAT = 2 * CONV_DIM
_S_KVLAT = _S_QLAT + Q_LORA_RANK
_S_KR = _S_KVLAT + KV_LORA_RANK
_S_GATE = _S_KR + QK_ROPE_DIM


def _prep_w_in_kernel(wt_ref, o_ref):
    def move(dst, src, n):
        for off in range(0, n, QK_PAD_DIM):
            m = min(QK_PAD_DIM, n - off)
            o_ref[:, dst + off:dst + off + m] = wt_ref[src + off:src + off + m, :].T.astype(BF16)

    move(_C_GLU, 0, _S_QLAT)
    move(_C_GATE, _S_GATE, _C_QLAT - _C_GATE)
    move(_C_QLAT, _S_QLAT, Q_LORA_RANK)
    move(_C_KVLAT, _S_KVLAT, KV_LORA_RANK)
    kr = wt_ref[_S_KR:_S_KR + LANES, :].T
    lane = lax.broadcasted_iota(jnp.int32, kr.shape, 1)
    o_ref[:, _C_KR:_C_END] = jnp.where(lane < QK_ROPE_DIM, kr, pltpu.roll(kr, QK_ROPE_DIM, 1)).astype(BF16)


def _prep_w_in(w):
    k, n = w.shape
    return pl.pallas_call(
        _prep_w_in_kernel,
        grid=(1,),
        in_specs=[_const_spec((n, k))],
        out_specs=pl.BlockSpec((k, _C_END), lambda i: (0, 0)),
        out_shape=jax.ShapeDtypeStruct((k, _C_END), BF16),
        compiler_params=_params("arbitrary"),
        name="prep_w_in",
    )(w.T)


def _rope_pairs(t, cos, sin, first_half):
    rot = jnp.where(first_half, -pltpu.roll(t, LANES - QK_ROPE_DIM // 2, 1),
                    pltpu.roll(t, QK_ROPE_DIM // 2, 1))
    return t * cos + rot * sin


def _in_conv_kernel(x_ref, mod_ref, gpre_ref, win_ref, qg_ref, wuq_ref, kvg_ref, wuk_ref, wuvt_ref,
                    pos_ref, invf_ref, cw_ref, cb_ref, lng_ref, lnb_ref, wco_ref,
                    ya_ref, gate_ref, q_ref, k_ref, vt_ref, buf_ref, act_ref, *, q_scale, tm, nt, rows):
    t = pl.program_id(0) % nt

    @pl.when(t == 0)
    def _():
        buf_ref[0:CONV_HALO] = jnp.zeros((CONV_HALO, SUBLANES, LANES), F32)

    x = x_ref[...]
    shift = mod_ref[0, 0:1, :]
    scale = mod_ref[0, 1:2, :]
    h = (_rms(x) * gpre_ref[...]) * (1.0 + scale) + shift
    hb = h.astype(BF16)

    zg = _dot(hb, win_ref[:, _C_GLU:_C_GATE])
    u = zg[:, :CONV_DIM] * jax.nn.sigmoid(zg[:, CONV_DIM:])
    buf_ref[CONV_HALO:] = u.reshape(tm, SUBLANES, LANES)
    first = CONV_HALO - (CONV_KERNEL - 1)

    def conv_chunk(r0):
        acc = jnp.broadcast_to(cb_ref[...][None], (rows, SUBLANES, LANES))
        for k in range(CONV_KERNEL):
            acc = acc + cw_ref[k][None] * buf_ref[r0 + first + k:r0 + first + k + rows]
        acc = acc.reshape(rows, CONV_DIM)
        mu = jnp.mean(acc, axis=-1, keepdims=True)
        cen = acc - mu
        var = jnp.mean(cen * cen, axis=-1, keepdims=True)
        y = cen * lax.rsqrt(var + EPS) * lng_ref[...] + lnb_ref[...]
        y = y * jax.nn.sigmoid(y)
        act_ref[r0:r0 + rows, :] = y.astype(BF16)

    chunk_starts = list(range(0, tm, rows))
    per_stage = -(-len(chunk_starts) // 4)

    def conv_stage(i):
        for r0 in chunk_starts[i * per_stage:(i + 1) * per_stage]:
            conv_chunk(r0)

    gate_ref[...] = jax.nn.sigmoid(_dot(hb, win_ref[:, _C_GATE:_C_QLAT])).astype(BF16)
    conv_stage(0)

    ang_t = invf_ref[...] * pos_ref[0]
    reps = LANES // (QK_ROPE_DIM // 2)
    cos = jnp.concatenate([jnp.cos(ang_t)] * reps, axis=0).T
    sin = jnp.concatenate([jnp.sin(ang_t)] * reps, axis=0).T
    lane = lax.broadcasted_iota(jnp.int32, cos.shape, 1)
    first_half = (lane % QK_ROPE_DIM) < (QK_ROPE_DIM // 2)
    low = lane < QK_ROPE_DIM

    ql = _dot(hb, win_ref[:, _C_QLAT:_C_KVLAT])
    qn = (_rms(ql) * qg_ref[...]).astype(BF16)
    q = _dot(qn, wuq_ref[...]) * q_scale
    nope_w = N_HEADS * QK_NOPE_DIM
    for hp in range(N_HEADS // 2):
        qr = _rope_pairs(q[:, nope_w + LANES * hp: nope_w + LANES * (hp + 1)], cos, sin, first_half)
        qr_hi = pltpu.roll(qr, QK_ROPE_DIM, 1)
        for j, part in ((0, qr), (1, qr_hi)):
            hd = 2 * hp + j
            q_ref[0, hd, :, 0:LANES] = q[:, LANES * hd: LANES * (hd + 1)].astype(BF16)
            q_ref[0, hd, :, LANES:QK_PAD_DIM] = jnp.where(low, part, 0.0).astype(BF16)
    conv_stage(1)

    kvl = _dot(hb, win_ref[:, _C_KVLAT:_C_KR])
    kvn = (_rms(kvl) * kvg_ref[...]).astype(BF16)
    kn = _dot(kvn, wuk_ref[...])
    kr = _rope_pairs(_dot(hb, win_ref[:, _C_KR:_C_END]), cos, sin, first_half)
    kr = jnp.where(low, kr, 0.0).astype(BF16)
    for hd in range(N_HEADS):
        k_ref[0, hd, :, 0:LANES] = kn[:, LANES * hd: LANES * (hd + 1)].astype(BF16)
        k_ref[0, hd, :, LANES:QK_PAD_DIM] = kr
    conv_stage(2)

    vt = lax.dot_general(wuvt_ref[...], kvn, (((1,), (1,)), ((), ())), preferred_element_type=F32)
    for hd in range(N_HEADS):
        vt_ref[0, hd] = vt[V_HEAD_DIM * hd: V_HEAD_DIM * (hd + 1), :].astype(BF16)
    conv_stage(3)

    ya_ref[...] = _dot(act_ref[...], wco_ref[...]).astype(BF16)
    buf_ref[0:CONV_HALO] = buf_ref[tm:tm + CONV_HALO]


def _in_conv(x2, mod3, g_pre, w_in_r, q_norm_g, w_uq_r, kv_norm_g, w_uk, w_uvt, pos_f, inv_freq,
             conv_w8, conv_b8, ln_g, ln_b, w_conv_out, batch, seq, tm, rows):
    nt = seq // tm
    q_scale = (1.0 / math.sqrt(QK_NOPE_DIM + QK_ROPE_DIM)) * math.log2(math.e)
    tok = lambda i: (i, 0)
    heads = lambda i: (i // nt, 0, i % nt, 0)
    return pl.pallas_call(
        functools.partial(_in_conv_kernel, q_scale=q_scale, tm=tm, nt=nt, rows=rows),
        grid=(batch * nt,),
        in_specs=[
            pl.BlockSpec((tm, D_MODEL), tok),
            pl.BlockSpec((1, 6, D_MODEL), lambda i: (i // nt, 0, 0)),
            _const_spec((1, D_MODEL)),
            _const_spec(w_in_r.shape),
            _const_spec((1, Q_LORA_RANK)),
            _const_spec(w_uq_r.shape),
            _const_spec((1, KV_LORA_RANK)),
            _const_spec(w_uk.shape),
            _const_spec(w_uvt.shape),
            pl.BlockSpec((1, 1, tm), lambda i: (i, 0, 0)),
            _const_spec((QK_ROPE_DIM // 2, 1)),
            _const_spec(conv_w8.shape),
            _const_spec((SUBLANES, LANES)),
            _const_spec((1, CONV_DIM)),
            _const_spec((1, CONV_DIM)),
            _const_spec(w_conv_out.shape),
        ],
        out_specs=[
            pl.BlockSpec((tm, D_MODEL), tok),
            pl.BlockSpec((tm, 2 * D_MODEL), tok),
            pl.BlockSpec((1, N_HEADS, tm, QK_PAD_DIM), heads),
            pl.BlockSpec((1, N_HEADS, tm, QK_PAD_DIM), heads),
            pl.BlockSpec((1, N_HEADS, V_HEAD_DIM, tm), lambda i: (i // nt, 0, 0, i % nt)),
        ],
        out_shape=[
            jax.ShapeDtypeStruct((batch * seq, D_MODEL), BF16),
            jax.ShapeDtypeStruct((batch * seq, 2 * D_MODEL), BF16),
            jax.ShapeDtypeStruct((batch, N_HEADS, seq, QK_PAD_DIM), BF16),
            jax.ShapeDtypeStruct((batch, N_HEADS, seq, QK_PAD_DIM), BF16),
            jax.ShapeDtypeStruct((batch, N_HEADS, V_HEAD_DIM, seq), BF16),
        ],
        scratch_shapes=[pltpu.VMEM((CONV_HALO + tm, SUBLANES, LANES), F32),
                        pltpu.VMEM((tm, CONV_DIM), BF16)],
        compiler_params=_params("arbitrary"),
        name="in_conv",
    )(x2, mod3, g_pre, w_in_r, q_norm_g, w_uq_r, kv_norm_g, w_uk, w_uvt,
      pos_f.reshape(batch * nt, 1, tm), inv_freq,
      conv_w8, conv_b8, ln_g, ln_b, w_conv_out)


def _attn_kernel(q_ref, k_ref, vt_ref, *rest, tq, tk_full, tk_diag, hpb, n_weights):
    w_refs, o_ref, wb_refs = rest[:n_weights], rest[n_weights], rest[n_weights + 1:2 * n_weights + 1]
    m_ref, acc_ref, s_ref = rest[2 * n_weights + 1:]
    for w_ref, wb_ref in zip(w_refs, wb_refs):
        wb_ref[...] = w_ref[...].astype(BF16)
    qi = pl.program_id(2)
    n_full = qi * (tq // tk_full)
    n_diag = tq // tk_diag
    diag_start = qi * tq
    m_ref[...] = jnp.full(m_ref.shape, -1e30, F32)
    acc_ref[...] = jnp.zeros(acc_ref.shape, F32)

    def scores(hd, start, tk, q0=0):
        k = k_ref[0, hd, pl.ds(start, tk), :]
        s_ref[hd, :tk, q0:] = lax.dot_general(k, q_ref[0, hd, q0:, :], (((1,), (1,)), ((), ())),
                                              preferred_element_type=F32)

    def update(hd, start, tk, q0=0, diag=False):
        s = s_ref[hd, :tk, q0:]
        if diag:
            kpos = lax.broadcasted_iota(jnp.int32, (tk, tk), 0)
            qpos = lax.broadcasted_iota(jnp.int32, (tk, tk), 1)
            tri = jnp.where(qpos >= kpos, s[:, :tk], -1e30)
            s = tri if q0 + tk == tq else jnp.concatenate([tri, s[:, tk:]], axis=1)
        m_old = m_ref[hd, :, q0:]
        m_new = jnp.maximum(m_old, jnp.max(s, axis=0, keepdims=True))
        alpha = jnp.exp2(m_old - m_new)
        p = jnp.exp2(s - m_new)
        m_ref[hd, :, q0:] = m_new
        vt = jnp.concatenate([vt_ref[0, hd, :, pl.ds(start, tk)], jnp.ones((DENOM_ROWS, tk), BF16)], axis=0)
        acc_ref[hd, :, q0:] = alpha * acc_ref[hd, :, q0:] + _dot(vt, p.astype(BF16))

    def full_start(j):
        return pl.multiple_of(j * tk_full, tk_full)

    def diag_tile(d):
        return pl.multiple_of(diag_start + d * tk_diag, tk_diag), tk_diag, d * tk_diag

    if tk_full == tk_diag:
        scores(0, full_start(0), tk_full)
    else:
        pl.when(n_full > 0)(lambda: scores(0, full_start(0), tk_full))
        pl.when(n_full == 0)(lambda: scores(0, *diag_tile(0)))

    def full_round(j, next_scores):
        for hd in range(hpb):
            if hd + 1 < hpb:
                scores(hd + 1, full_start(j), tk_full)
            else:
                next_scores()
            update(hd, full_start(j), tk_full)

    def body(j, carry):
        full_round(j, lambda: scores(0, full_start(j + 1), tk_full))
        return carry

    lax.fori_loop(0, jnp.maximum(n_full - 1, 0), body, 0)
    pl.when(n_full > 0)(lambda: full_round(n_full - 1, lambda: scores(0, *diag_tile(0))))
    for d in range(n_diag):
        for hd in range(hpb):
            if hd + 1 < hpb:
                scores(hd + 1, *diag_tile(d))
            elif d + 1 < n_diag:
                scores(0, *diag_tile(d + 1))
            update(hd, *diag_tile(d), True)
    for hd in range(hpb):
        o = acc_ref[hd, :V_HEAD_DIM, :] / acc_ref[hd, V_HEAD_DIM:V_HEAD_DIM + 1, :]
        o_ref[0, :, V_HEAD_DIM * hd: V_HEAD_DIM * (hd + 1)] = o.T.astype(BF16)


def _attn(q4, k4, vt4, later_weights, tq, tk_full, tk_diag, hpb):
    batch, heads, seq, _ = q4.shape
    grid = (batch, heads // hpb, seq // tq)
    steps = grid[0] * grid[1] * grid[2]
    slab = lambda b, h, i: ((b * grid[1] + h) * grid[2] + i, 0)
    w_specs = [pl.BlockSpec((w.shape[0] // steps, w.shape[1]), slab) for w in later_weights]
    outs = pl.pallas_call(
        functools.partial(_attn_kernel, tq=tq, tk_full=tk_full, tk_diag=tk_diag, hpb=hpb,
                          n_weights=len(later_weights)),
        grid=grid,
        in_specs=[
            pl.BlockSpec((1, hpb, tq, QK_PAD_DIM), lambda b, h, i: (b, h, i, 0)),
            pl.BlockSpec((1, hpb, seq, QK_PAD_DIM), lambda b, h, i: (b, h, 0, 0)),
            pl.BlockSpec((1, hpb, V_HEAD_DIM, seq), lambda b, h, i: (b, h, 0, 0)),
        ] + w_specs,
        out_specs=[pl.BlockSpec((1, tq, hpb * V_HEAD_DIM), lambda b, h, i: (b, i, h))] + w_specs,
        out_shape=[jax.ShapeDtypeStruct((batch, seq, heads * V_HEAD_DIM), BF16)]
        + [jax.ShapeDtypeStruct(w.shape, BF16) for w in later_weights],
        scratch_shapes=[pltpu.VMEM((hpb, 1, tq), F32),
                        pltpu.VMEM((hpb, V_HEAD_DIM + DENOM_ROWS, tq), F32),
                        pltpu.VMEM((hpb, max(tk_full, tk_diag), tq), F32)],
        compiler_params=_params("arbitrary", "arbitrary", "arbitrary"),
        name="attn",
    )(q4, k4, vt4, *later_weights)
    return outs[0], outs[1:]


def _mix_mlp_kernel(o_ref, ya_ref, gate_ref, x_ref, mod_ref, gpost_mix_ref, gpre_mlp_ref, gpost_mlp_ref,
                    wao_ref, wout_ref, w1_ref, w2_ref, out_ref):
    x = x_ref[...]
    y_b = _dot(o_ref[...], wao_ref[...])
    g_a = gate_ref[:, :D_MODEL].astype(F32)
    g_b = gate_ref[:, D_MODEL:].astype(F32)
    merged = g_a * ya_ref[...].astype(F32) + g_b * y_b
    y = _dot(merged.astype(BF16), wout_ref[...])
    x1 = x + mod_ref[0, 2:3, :] * (_rms(y) * gpost_mix_ref[...])

    shift = mod_ref[0, 3:4, :]
    scale = mod_ref[0, 4:5, :]
    h = (_rms(x1) * gpre_mlp_ref[...]) * (1.0 + scale) + shift
    hid = jnp.maximum(_dot(h.astype(BF16), w1_ref[...]), 0.0)
    y2 = _dot((hid * hid).astype(BF16), w2_ref[...])
    out_ref[...] = x1 + mod_ref[0, 5:6, :] * (_rms(y2) * gpost_mlp_ref[...])


def _mix_mlp(o2, ya2, gates, x2, mod3, g_post_mix, g_pre_mlp, g_post_mlp, w_attn_out, w_out, w1, w2, seq, tm):
    tokens = x2.shape[0]
    nt = seq // tm
    tok = lambda i: (i, 0)
    return pl.pallas_call(
        _mix_mlp_kernel,
        grid=(tokens // tm,),
        in_specs=[
            pl.BlockSpec((tm, D_MODEL), tok),
            pl.BlockSpec((tm, D_MODEL), tok),
            pl.BlockSpec((tm, 2 * D_MODEL), tok),
            pl.BlockSpec((tm, D_MODEL), tok),
            pl.BlockSpec((1, 6, D_MODEL), lambda i: (i // nt, 0, 0)),
            _const_spec((1, D_MODEL)),
            _const_spec((1, D_MODEL)),
            _const_spec((1, D_MODEL)),
            _const_spec(w_attn_out.shape),
            _const_spec(w_out.shape),
            _const_spec(w1.shape),
            _const_spec(w2.shape),
        ],
        out_specs=pl.BlockSpec((tm, D_MODEL), tok),
        out_shape=jax.ShapeDtypeStruct((tokens, D_MODEL), F32),
        compiler_params=_params("arbitrary"),
        name="mix_mlp",
    )(o2, ya2, gates, x2, mod3, g_post_mix, g_pre_mlp, g_post_mlp, w_attn_out, w_out, w1, w2)


def _head_major_cols(w, split):
    k = w.shape[0]
    w3 = w.reshape(k, N_HEADS, -1)
    return jnp.concatenate([w3[:, :, :split].reshape(k, -1), w3[:, :, split:].reshape(k, -1)], axis=1)


def kernel(x, c, positions, w_ada, b_ada, g_pre_mix, g_post_mix, g_pre_mlp, g_post_mlp, w_in, conv_w, conv_b, conv_norm_g, conv_norm_b, w_conv_out, q_norm_g, w_uq, kv_norm_g, w_ukv, w_attn_out, w_out, w_mlp_in, w_mlp_out):
    batch, seq, d = x.shape
    depth = w_ada.shape[0]
    tokens = batch * seq
    row = lambda v: v.reshape(1, -1)

    inv_freq = 1.0 / (ROPE_THETA ** (jnp.arange(0, QK_ROPE_DIM, 2, dtype=F32) / QK_ROPE_DIM))
    inv_freq = inv_freq.reshape(-1, 1)
    pos_f = positions.astype(F32)
    c_pad = jnp.pad(c, ((0, 8 - batch), (0, 0)))

    x2 = x.reshape(tokens, d)
    for l in range(depth):
        w_in_r = _prep_w_in(w_in[l])
        w_uq_r = _head_major_cols(w_uq[l], QK_NOPE_DIM).astype(BF16)
        w_ukv_r = _head_major_cols(w_ukv[l], QK_NOPE_DIM).astype(BF16)
        w_uk = w_ukv_r[:, :N_HEADS * QK_NOPE_DIM]
        w_uvt = w_ukv_r[:, N_HEADS * QK_NOPE_DIM:].T

        mod = _ada(c_pad, w_ada[l], row(b_ada[l]))
        mod3 = mod[:batch].reshape(batch, 6, d)

        conv_w8 = conv_w[l].reshape(CONV_KERNEL, SUBLANES, LANES)
        conv_b8 = conv_b[l].reshape(SUBLANES, LANES)
        y_a, gates, q4, k4, vt4 = _in_conv(x2, mod3, row(g_pre_mix[l]), w_in_r, row(q_norm_g[l]), w_uq_r,
                                           row(kv_norm_g[l]), w_uk, w_uvt, pos_f, inv_freq, conv_w8, conv_b8,
                                           row(conv_norm_g[l]), row(conv_norm_b[l]), w_conv_out[l].astype(BF16),
                                           batch, seq, tm=IN_CONV_ROWS, rows=CONV_CHUNK_ROWS)
        o, (w_ao, w_o, w_1, w_2) = _attn(q4, k4, vt4, (w_attn_out[l], w_out[l], w_mlp_in[l], w_mlp_out[l]),
                                         tq=ATTN_Q_ROWS, tk_full=ATTN_K_ROWS, tk_diag=ATTN_K_ROWS,
                                         hpb=ATTN_HEADS_PER_STEP)
        x2 = _mix_mlp(o.reshape(tokens, d), y_a, gates, x2, mod3, row(g_post_mix[l]), row(g_pre_mlp[l]),
                      row(g_post_mlp[l]), w_ao, w_o, w_1, w_2, seq, tm=MIX_MLP_ROWS)
    return x2.reshape(batch, seq, d)
```

```python
import functools
import math

import jax
import jax.numpy as jnp
from jax import lax
from jax.experimental import pallas as pl
from jax.experimental.pallas import tpu as pltpu

D_MODEL = 1024
CONV_DIM = 1024
CONV_KERNEL = 31
N_HEADS = 8
QK_NOPE_DIM = 128
QK_ROPE_DIM = 64
V_HEAD_DIM = 128
Q_LORA_RANK = 384
KV_LORA_RANK = 256
ROPE_THETA = 10000.0
EPS = 1e-6

LANES = 128
SUBLANES = 8
QK_PAD_DIM = 2 * LANES
DENOM_ROWS = 16
CONV_HALO = 32
VMEM_LIMIT = 56 * 1024 * 1024

IN_CONV_ROWS = 512
CONV_CHUNK_ROWS = 32
ATTN_Q_ROWS = 1024
ATTN_K_ROWS = 512
ATTN_HEADS_PER_STEP = 4
MIX_MLP_ROWS = 512

_C_GLU = 0
_C_GATE = 2 * CONV_DIM
_C_QLAT = _C_GATE + 2 * D_MODEL
_C_KVLAT = _C_QLAT + Q_LORA_RANK
_C_KR = _C_KVLAT + KV_LORA_RANK
_C_END = _C_KR + LANES

BF16 = jnp.bfloat16
F32 = jnp.float32


def _dot(a, b):
    return jnp.dot(a, b, preferred_element_type=F32)


def _rms(v):
    return v * lax.rsqrt(jnp.mean(v * v, axis=-1, keepdims=True) + EPS)


def _params(*sem):
    return pltpu.CompilerParams(dimension_semantics=sem, vmem_limit_bytes=VMEM_LIMIT)


def _const_spec(shape):
    nd = len(shape)
    return pl.BlockSpec(shape, lambda *_: (0,) * nd, pipeline_mode=pl.Buffered(1))


def _ada_kernel(c_ref, w_ref, b_ref, o_ref):
    c = c_ref[...]
    act = c * jax.nn.sigmoid(c)
    o_ref[...] = _dot(act.astype(BF16), w_ref[...].astype(BF16)) + b_ref[...]


def _ada(c_pad, w_ada, b_ada):
    rows, d = c_pad.shape
    n = w_ada.shape[1]
    tn = 1024
    return pl.pallas_call(
        _ada_kernel,
        grid=(n // tn,),
        in_specs=[
            pl.BlockSpec((rows, d), lambda j: (0, 0)),
            pl.BlockSpec((d, tn), lambda j: (0, j)),
            pl.BlockSpec((1, tn), lambda j: (0, j)),
        ],
        out_specs=pl.BlockSpec((rows, tn), lambda j: (0, j)),
        out_shape=jax.ShapeDtypeStruct((rows, n), F32),
        compiler_params=_params("arbitrary"),
        name="ada",
    )(c_pad, w_ada, b_ada)


_S_QLAT = 2 * CONV_DIM
_S_KVLAT = _S_QLAT + Q_LORA_RANK
_S_KR = _S_KVLAT + KV_LORA_RANK
_S_GATE = _S_KR + QK_ROPE_DIM


def _prep_w_in_kernel(wt_ref, o_ref):
    def move(dst, src, n):
        for off in range(0, n, QK_PAD_DIM):
            m = min(QK_PAD_DIM, n - off)
            o_ref[:, dst + off:dst + off + m] = wt_ref[src + off:src + off + m, :].T.astype(BF16)

    move(_C_GLU, 0, _S_QLAT)
    move(_C_GATE, _S_GATE, _C_QLAT - _C_GATE)
    move(_C_QLAT, _S_QLAT, Q_LORA_RANK)
    move(_C_KVLAT, _S_KVLAT, KV_LORA_RANK)
    kr = wt_ref[_S_KR:_S_KR + LANES, :].T
    lane = lax.broadcasted_iota(jnp.int32, kr.shape, 1)
    o_ref[:, _C_KR:_C_END] = jnp.where(lane < QK_ROPE_DIM, kr, pltpu.roll(kr, QK_ROPE_DIM, 1)).astype(BF16)


def _prep_w_in(w):
    k, n = w.shape
    return pl.pallas_call(
        _prep_w_in_kernel,
        grid=(1,),
        in_specs=[_const_spec((n, k))],
        out_specs=pl.BlockSpec((k, _C_END), lambda i: (0, 0)),
        out_shape=jax.ShapeDtypeStruct((k, _C_END), BF16),
        compiler_params=_params("arbitrary"),
        name="prep_w_in",
    )(w.T)


def _rope_pairs(t, cos, sin, first_half):
    rot = jnp.where(first_half, -pltpu.roll(t, LANES - QK_ROPE_DIM // 2, 1),
                    pltpu.roll(t, QK_ROPE_DIM // 2, 1))
    return t * cos + rot * sin


def _in_conv_kernel(x_ref, mod_ref, gpre_ref, win_ref, qg_ref, wuqt_ref, kvg_ref, wuk_ref, wuvt_ref,
                    pos_ref, invf_ref, cw_ref, cb_ref, lng_ref, lnb_ref, wco_ref,
                    ya_ref, gate_ref, q_ref, k_ref, vt_ref, buf_ref, act_ref, *, q_scale, tm, nt, rows):
    t = pl.program_id(0) % nt

    @pl.when(t == 0)
    def _():
        buf_ref[0:CONV_HALO] = jnp.zeros((CONV_HALO, SUBLANES, LANES), F32)

    x = x_ref[...]
    shift = mod_ref[0, 0:1, :]
    scale = mod_ref[0, 1:2, :]
    h = (_rms(x) * gpre_ref[...]) * (1.0 + scale) + shift
    hb = h.astype(BF16)

    zg = _dot(hb, win_ref[:, _C_GLU:_C_GATE])
    u = zg[:, :CONV_DIM] * jax.nn.sigmoid(zg[:, CONV_DIM:])
    buf_ref[CONV_HALO:] = u.reshape(tm, SUBLANES, LANES)
    first = CONV_HALO - (CONV_KERNEL - 1)

    def conv_chunk(r0):
        acc = jnp.broadcast_to(cb_ref[...][None], (rows, SUBLANES, LANES))
        for k in range(CONV_KERNEL):
            acc = acc + cw_ref[k][None] * buf_ref[r0 + first + k:r0 + first + k + rows]
        acc = acc.reshape(rows, CONV_DIM)
        mu = jnp.mean(acc, axis=-1, keepdims=True)
        cen = acc - mu
        var = jnp.mean(cen * cen, axis=-1, keepdims=True)
        y = cen * lax.rsqrt(var + EPS) * lng_ref[...] + lnb_ref[...]
        y = y * jax.nn.sigmoid(y)
        act_ref[r0:r0 + rows, :] = y.astype(BF16)

    chunk_starts = list(range(0, tm, rows))
    per_stage = -(-len(chunk_starts) // 4)

    def conv_stage(i):
        for r0 in chunk_starts[i * per_stage:(i + 1) * per_stage]:
            conv_chunk(r0)

    gate_ref[...] = _dot(hb, win_ref[:, _C_GATE:_C_QLAT]).astype(BF16)
    conv_stage(0)

    ang_t = invf_ref[...] * pos_ref[0]
    cos_t = jnp.cos(ang_t)
    sin_t = jnp.sin(ang_t)
    reps = LANES // (QK_ROPE_DIM // 2)
    cos = jnp.concatenate([cos_t] * reps, axis=0).T
    sin = jnp.concatenate([sin_t] * reps, axis=0).T
    lane = lax.broadcasted_iota(jnp.int32, cos.shape, 1)
    first_half = (lane % QK_ROPE_DIM) < (QK_ROPE_DIM // 2)
    low = lane < QK_ROPE_DIM

    ql = _dot(hb, win_ref[:, _C_QLAT:_C_KVLAT])
    qn = (_rms(ql) * qg_ref[...]).astype(BF16)
    qt = lax.dot_general(wuqt_ref[...], qn, (((1,), (1,)), ((), ())), preferred_element_type=F32) * q_scale
    nope_w = N_HEADS * QK_NOPE_DIM
    half_r = QK_ROPE_DIM // 2
    for hd in range(N_HEADS):
        q_ref[0, hd, 0:QK_NOPE_DIM, :] = qt[QK_NOPE_DIM * hd: QK_NOPE_DIM * (hd + 1)].astype(BF16)
        t1 = qt[nope_w + QK_ROPE_DIM * hd: nope_w + QK_ROPE_DIM * hd + half_r]
        t2 = qt[nope_w + QK_ROPE_DIM * hd + half_r: nope_w + QK_ROPE_DIM * (hd + 1)]
        q_ref[0, hd, QK_NOPE_DIM:QK_NOPE_DIM + half_r, :] = (t1 * cos_t - t2 * sin_t).astype(BF16)
        q_ref[0, hd, QK_NOPE_DIM + half_r:QK_NOPE_DIM + QK_ROPE_DIM, :] = (t2 * cos_t + t1 * sin_t).astype(BF16)
        q_ref[0, hd, QK_NOPE_DIM + QK_ROPE_DIM:, :] = jnp.zeros((QK_PAD_DIM - QK_NOPE_DIM - QK_ROPE_DIM, tm), BF16)
    conv_stage(1)

    kvl = _dot(hb, win_ref[:, _C_KVLAT:_C_KR])
    kvn = (_rms(kvl) * kvg_ref[...]).astype(BF16)
    kn = _dot(kvn, wuk_ref[...])
    kr = _rope_pairs(_dot(hb, win_ref[:, _C_KR:_C_END]), cos, sin, first_half)
    kr = jnp.where(low, kr, 0.0).astype(BF16)
    for hd in range(N_HEADS):
        k_ref[0, hd, :, 0:LANES] = kn[:, LANES * hd: LANES * (hd + 1)].astype(BF16)
        k_ref[0, hd, :, LANES:QK_PAD_DIM] = kr
    conv_stage(2)

    vt = lax.dot_general(wuvt_ref[...], kvn, (((1,), (1,)), ((), ())), preferred_element_type=F32)
    for hd in range(N_HEADS):
        vt_ref[0, hd] = vt[V_HEAD_DIM * hd: V_HEAD_DIM * (hd + 1), :].astype(BF16)
    conv_stage(3)

    ya_ref[...] = _dot(act_ref[...], wco_ref[...]).astype(BF16)
    buf_ref[0:CONV_HALO] = buf_ref[tm:tm + CONV_HALO]


def _in_conv(x2, mod3, g_pre, w_in_r, q_norm_g, w_uqt, kv_norm_g, w_uk, w_uvt, pos_f, inv_freq,
             conv_w8, conv_b8, ln_g, ln_b, w_conv_out, batch, seq, tm, rows):
    nt = seq // tm
    q_scale = (1.0 / math.sqrt(QK_NOPE_DIM + QK_ROPE_DIM)) * math.log2(math.e)
    tok = lambda i: (i, 0)
    heads = lambda i: (i // nt, 0, i % nt, 0)
    return pl.pallas_call(
        functools.partial(_in_conv_kernel, q_scale=q_scale, tm=tm, nt=nt, rows=rows),
        grid=(batch * nt,),
        in_specs=[
            pl.BlockSpec((tm, D_MODEL), tok),
            pl.BlockSpec((1, 6, D_MODEL), lambda i: (i // nt, 0, 0)),
            _const_spec((1, D_MODEL)),
            _const_spec(w_in_r.shape),
            _const_spec((1, Q_LORA_RANK)),
            _const_spec(w_uqt.shape),
            _const_spec((1, KV_LORA_RANK)),
            _const_spec(w_uk.shape),
            _const_spec(w_uvt.shape),
            pl.BlockSpec((1, 1, tm), lambda i: (i, 0, 0)),
            _const_spec((QK_ROPE_DIM // 2, 1)),
            _const_spec(conv_w8.shape),
            _const_spec((SUBLANES, LANES)),
            _const_spec((1, CONV_DIM)),
            _const_spec((1, CONV_DIM)),
            _const_spec(w_conv_out.shape),
        ],
        out_specs=[
            pl.BlockSpec((tm, D_MODEL), tok),
            pl.BlockSpec((tm, 2 * D_MODEL), tok),
            pl.BlockSpec((1, N_HEADS, QK_PAD_DIM, tm), lambda i: (i // nt, 0, 0, i % nt)),
            pl.BlockSpec((1, N_HEADS, tm, QK_PAD_DIM), heads),
            pl.BlockSpec((1, N_HEADS, V_HEAD_DIM, tm), lambda i: (i // nt, 0, 0, i % nt)),
        ],
        out_shape=[
            jax.ShapeDtypeStruct((batch * seq, D_MODEL), BF16),
            jax.ShapeDtypeStruct((batch * seq, 2 * D_MODEL), BF16),
            jax.ShapeDtypeStruct((batch, N_HEADS, QK_PAD_DIM, seq), BF16),
            jax.ShapeDtypeStruct((batch, N_HEADS, seq, QK_PAD_DIM), BF16),
            jax.ShapeDtypeStruct((batch, N_HEADS, V_HEAD_DIM, seq), BF16),
        ],
        scratch_shapes=[pltpu.VMEM((CONV_HALO + tm, SUBLANES, LANES), F32),
                        pltpu.VMEM((tm, CONV_DIM), BF16)],
        compiler_params=_params("arbitrary"),
        name="in_conv",
    )(x2, mod3, g_pre, w_in_r, q_norm_g, w_uqt, kv_norm_g, w_uk, w_uvt,
      pos_f.reshape(batch * nt, 1, tm), inv_freq,
      conv_w8, conv_b8, ln_g, ln_b, w_conv_out)


def _attn_kernel(qt_ref, k_ref, vt_ref, *rest, tq, tk, hpb, n_weights):
    w_refs, o_ref, wb_refs = rest[:n_weights], rest[n_weights], rest[n_weights + 1:2 * n_weights + 1]
    m_ref, acc_ref, s_ref = rest[2 * n_weights + 1:]
    for w_ref, wb_ref in zip(w_refs, wb_refs):
        wb_ref[...] = w_ref[...].astype(BF16)
    qi = pl.program_id(2)
    n_diag = tq // tk
    n_full = qi * n_diag
    m_ref[...] = jnp.full(m_ref.shape, -1e30, F32)
    acc_ref[...] = jnp.zeros(acc_ref.shape, F32)

    def tile_start(j):
        return pl.multiple_of(j * tk, tk)

    def scores(hd, j, q0=0):
        k = k_ref[0, hd, pl.ds(tile_start(j), tk), :]
        s_ref[hd, :, q0:] = _dot(k, qt_ref[0, hd, :, q0:])

    def update(hd, j, q0=0, diag=False):
        s = s_ref[hd, :, q0:]
        if diag:
            kpos = lax.broadcasted_iota(jnp.int32, (tk, tk), 0)
            qpos = lax.broadcasted_iota(jnp.int32, (tk, tk), 1)
            tri = jnp.where(qpos >= kpos, s[:, :tk], -1e30)
            s = tri if q0 + tk == tq else jnp.concatenate([tri, s[:, tk:]], axis=1)
        m_old = m_ref[hd, :, q0:]
        m_new = jnp.maximum(m_old, jnp.max(s, axis=0, keepdims=True))
        alpha = jnp.exp2(m_old - m_new)
        p = jnp.exp2(s - m_new)
        m_ref[hd, :, q0:] = m_new
        vt = jnp.concatenate([vt_ref[0, hd, :, pl.ds(tile_start(j), tk)], jnp.ones((DENOM_ROWS, tk), BF16)],
                             axis=0)
        acc_ref[hd, :, q0:] = alpha * acc_ref[hd, :, q0:] + _dot(vt, p.astype(BF16))

    scores(0, 0)

    def full_round(j, carry):
        for hd in range(hpb):
            if hd + 1 < hpb:
                scores(hd + 1, j)
            else:
                scores(0, j + 1)
            update(hd, j)
        return carry

    lax.fori_loop(0, n_full, full_round, 0)
    for d in range(n_diag):
        for hd in range(hpb):
            if hd + 1 < hpb:
                scores(hd + 1, n_full + d, d * tk)
            elif d + 1 < n_diag:
                scores(0, n_full + d + 1, (d + 1) * tk)
            update(hd, n_full + d, d * tk, True)
    for hd in range(hpb):
        o = acc_ref[hd, :V_HEAD_DIM, :] / acc_ref[hd, V_HEAD_DIM:V_HEAD_DIM + 1, :]
        o_ref[0, :, V_HEAD_DIM * hd: V_HEAD_DIM * (hd + 1)] = o.T.astype(BF16)


def _attn(qt4, k4, vt4, later_weights, tq, tk, hpb):
    batch, heads, seq, _ = k4.shape
    grid = (batch, heads // hpb, seq // tq)
    steps = grid[0] * grid[1] * grid[2]
    slab = lambda b, h, i: ((b * grid[1] + h) * grid[2] + i, 0)
    w_specs = [pl.BlockSpec((w.shape[0] // steps, w.shape[1]), slab) for w in later_weights]
    outs = pl.pallas_call(
        functools.partial(_attn_kernel, tq=tq, tk=tk, hpb=hpb,
                          n_weights=len(later_weights)),
        grid=grid,
        in_specs=[
            pl.BlockSpec((1, hpb, QK_PAD_DIM, tq), lambda b, h, i: (b, h, 0, i)),
            pl.BlockSpec((1, hpb, seq, QK_PAD_DIM), lambda b, h, i: (b, h, 0, 0)),
            pl.BlockSpec((1, hpb, V_HEAD_DIM, seq), lambda b, h, i: (b, h, 0, 0)),
        ] + w_specs,
        out_specs=[pl.BlockSpec((1, tq, hpb * V_HEAD_DIM), lambda b, h, i: (b, i, h))] + w_specs,
        out_shape=[jax.ShapeDtypeStruct((batch, seq, heads * V_HEAD_DIM), BF16)]
        + [jax.ShapeDtypeStruct(w.shape, BF16) for w in later_weights],
        scratch_shapes=[pltpu.VMEM((hpb, 1, tq), F32),
                        pltpu.VMEM((hpb, V_HEAD_DIM + DENOM_ROWS, tq), F32),
                        pltpu.VMEM((hpb, tk, tq), F32)],
        compiler_params=_params("arbitrary", "arbitrary", "arbitrary"),
        name="attn",
    )(qt4, k4, vt4, *later_weights)
    return outs[0], outs[1:]


def _mix_mlp_kernel(o_ref, ya_ref, gate_ref, x_ref, mod_ref, gpost_mix_ref, gpre_mlp_ref, gpost_mlp_ref,
                    wao_ref, wout_ref, w1_ref, w2_ref, out_ref):
    x = x_ref[...]
    y_b = _dot(o_ref[...], wao_ref[...])
    g_a = jax.nn.sigmoid(gate_ref[:, :D_MODEL].astype(F32))
    g_b = jax.nn.sigmoid(gate_ref[:, D_MODEL:].astype(F32))
    merged = g_a * ya_ref[...].astype(F32) + g_b * y_b
    y = _dot(merged.astype(BF16), wout_ref[...])
    x1 = x + mod_ref[0, 2:3, :] * (_rms(y) * gpost_mix_ref[...])

    shift = mod_ref[0, 3:4, :]
    scale = mod_ref[0, 4:5, :]
    h = (_rms(x1) * gpre_mlp_ref[...]) * (1.0 + scale) + shift
    hid = jnp.maximum(_dot(h.astype(BF16), w1_ref[...]), 0.0)
    y2 = _dot((hid * hid).astype(BF16), w2_ref[...])
    out_ref[...] = x1 + mod_ref[0, 5:6, :] * (_rms(y2) * gpost_mlp_ref[...])


def _mix_mlp(o2, ya2, gates, x2, mod3, g_post_mix, g_pre_mlp, g_post_mlp, w_attn_out, w_out, w1, w2, seq, tm):
    tokens = x2.shape[0]
    nt = seq // tm
    tok = lambda i: (i, 0)
    return pl.pallas_call(
        _mix_mlp_kernel,
        grid=(tokens // tm,),
        in_specs=[
            pl.BlockSpec((tm, D_MODEL), tok),
            pl.BlockSpec((tm, D_MODEL), tok),
            pl.BlockSpec((tm, 2 * D_MODEL), tok),
            pl.BlockSpec((tm, D_MODEL), tok),
            pl.BlockSpec((1, 6, D_MODEL), lambda i: (i // nt, 0, 0)),
            _const_spec((1, D_MODEL)),
            _const_spec((1, D_MODEL)),
            _const_spec((1, D_MODEL)),
            _const_spec(w_attn_out.shape),
            _const_spec(w_out.shape),
            _const_spec(w1.shape),
            _const_spec(w2.shape),
        ],
        out_specs=pl.BlockSpec((tm, D_MODEL), tok),
        out_shape=jax.ShapeDtypeStruct((tokens, D_MODEL), F32),
        compiler_params=_params("arbitrary"),
        name="mix_mlp",
    )(o2, ya2, gates, x2, mod3, g_post_mix, g_pre_mlp, g_post_mlp, w_attn_out, w_out, w1, w2)


def _head_major_cols(w, split):
    k = w.shape[0]
    w3 = w.reshape(k, N_HEADS, -1)
    return jnp.concatenate([w3[:, :, :split].reshape(k, -1), w3[:, :, split:].reshape(k, -1)], axis=1)


def kernel(x, c, positions, w_ada, b_ada, g_pre_mix, g_post_mix, g_pre_mlp, g_post_mlp, w_in, conv_w, conv_b, conv_norm_g, conv_norm_b, w_conv_out, q_norm_g, w_uq, kv_norm_g, w_ukv, w_attn_out, w_out, w_mlp_in, w_mlp_out):
    batch, seq, d = x.shape
    depth = w_ada.shape[0]
    tokens = batch * seq
    row = lambda v: v.reshape(1, -1)

    inv_freq = 1.0 / (ROPE_THETA ** (jnp.arange(0, QK_ROPE_DIM, 2, dtype=F32) / QK_ROPE_DIM))
    inv_freq = inv_freq.reshape(-1, 1)
    pos_f = positions.astype(F32)
    c_pad = jnp.pad(c, ((0, 8 - batch), (0, 0)))

    x2 = x.reshape(tokens, d)
    for l in range(depth):
        w_in_r = _prep_w_in(w_in[l])
        w_uqt = _head_major_cols(w_uq[l], QK_NOPE_DIM).astype(BF16).T
        w_ukv_r = _head_major_cols(w_ukv[l], QK_NOPE_DIM).astype(BF16)
        w_uk = w_ukv_r[:, :N_HEADS * QK_NOPE_DIM]
        w_uvt = w_ukv_r[:, N_HEADS * QK_NOPE_DIM:].T

        mod = _ada(c_pad, w_ada[l], row(b_ada[l]))
        mod3 = mod[:batch].reshape(batch, 6, d)

        conv_w8 = conv_w[l].reshape(CONV_KERNEL, SUBLANES, LANES)
        conv_b8 = conv_b[l].reshape(SUBLANES, LANES)
        y_a, gates, q4, k4, vt4 = _in_conv(x2, mod3, row(g_pre_mix[l]), w_in_r, row(q_norm_g[l]), w_uqt,
                                           row(kv_norm_g[l]), w_uk, w_uvt, pos_f, inv_freq, conv_w8, conv_b8,
                                           row(conv_norm_g[l]), row(conv_norm_b[l]), w_conv_out[l].astype(BF16),
                                           batch, seq, tm=IN_CONV_ROWS, rows=CONV_CHUNK_ROWS)
        o, (w_ao, w_o, w_1, w_2) = _attn(q4, k4, vt4, (w_attn_out[l], w_out[l], w_mlp_in[l], w_mlp_out[l]),
                                         tq=ATTN_Q_ROWS, tk=ATTN_K_ROWS, hpb=ATTN_HEADS_PER_STEP)
        x2 = _mix_mlp(o.reshape(tokens, d), y_a, gates, x2, mod3, row(g_post_mix[l]), row(g_pre_mlp[l]),
                      row(g_post_mlp[l]), w_ao, w_o, w_1, w_2, seq, tm=MIX_MLP_ROWS)
    return x2.reshape(batch, seq, d)
```

```python
import functools
import math

import jax
import jax.numpy as jnp
from jax import lax
from jax.experimental import pallas as pl
from jax.experimental.pallas import tpu as pltpu

D_MODEL = 1024
CONV_DIM = 1024
CONV_KERNEL = 31
N_HEADS = 8
QK_NOPE_DIM = 128
QK_ROPE_DIM = 64
V_HEAD_DIM = 128
Q_LORA_RANK = 384
KV_LORA_RANK = 256
ROPE_THETA = 10000.0
EPS = 1e-6

LANES = 128
SUBLANES = 8
QK_PAD_DIM = 2 * LANES
DENOM_ROWS = 16
CONV_HALO = 32
VMEM_LIMIT = 56 * 1024 * 1024

IN_CONV_ROWS = 512
CONV_CHUNK_ROWS = 32
ATTN_Q_ROWS = 1024
ATTN_K_ROWS = 512
ATTN_HEADS_PER_STEP = 4
MIX_MLP_ROWS = 512

_C_GLU = 0
_C_GATE = 2 * CONV_DIM
_C_QLAT = _C_GATE + 2 * D_MODEL
_C_KVLAT = _C_QLAT + Q_LORA_RANK
_C_KR = _C_KVLAT + KV_LORA_RANK
_C_END = _C_KR + LANES

BF16 = jnp.bfloat16
F32 = jnp.float32


def _dot(a, b):
    return jnp.dot(a, b, preferred_element_type=F32)


def _rms(v):
    return v * lax.rsqrt(jnp.mean(v * v, axis=-1, keepdims=True) + EPS)


def _params(*sem):
    return pltpu.CompilerParams(dimension_semantics=sem, vmem_limit_bytes=VMEM_LIMIT)


def _const_spec(shape):
    nd = len(shape)
    return pl.BlockSpec(shape, lambda *_: (0,) * nd, pipeline_mode=pl.Buffered(1))


def _ada_kernel(c_ref, w_ref, b_ref, o_ref):
    c = c_ref[...]
    act = c * jax.nn.sigmoid(c)
    o_ref[...] = _dot(act.astype(BF16), w_ref[...].astype(BF16)) + b_ref[...]


def _ada(c_pad, w_ada, b_ada):
    rows, d = c_pad.shape
    n = w_ada.shape[1]
    tn = 1024
    return pl.pallas_call(
        _ada_kernel,
        grid=(n // tn,),
        in_specs=[
            pl.BlockSpec((rows, d), lambda j: (0, 0)),
            pl.BlockSpec((d, tn), lambda j: (0, j)),
            pl.BlockSpec((1, tn), lambda j: (0, j)),
        ],
        out_specs=pl.BlockSpec((rows, tn), lambda j: (0, j)),
        out_shape=jax.ShapeDtypeStruct((rows, n), F32),
        compiler_params=_params("arbitrary"),
        name="ada",
    )(c_pad, w_ada, b_ada)


_S_QLAT = 2 * CONV_DIM
_S_KVLAT = _S_QLAT + Q_LORA_RANK
_S_KR = _S_KVLAT + KV_LORA_RANK
_S_GATE = _S_KR + QK_ROPE_DIM


def _prep_w_in_kernel(wt_ref, o_ref):
    def move(dst, src, n):
        for off in range(0, n, QK_PAD_DIM):
            m = min(QK_PAD_DIM, n - off)
            o_ref[:, dst + off:dst + off + m] = wt_ref[src + off:src + off + m, :].T.astype(BF16)

    move(_C_GLU, 0, _S_QLAT)
    move(_C_GATE, _S_GATE, _C_QLAT - _C_GATE)
    move(_C_QLAT, _S_QLAT, Q_LORA_RANK)
    move(_C_KVLAT, _S_KVLAT, KV_LORA_RANK)
    kr = wt_ref[_S_KR:_S_KR + LANES, :].T
    lane = lax.broadcasted_iota(jnp.int32, kr.shape, 1)
    o_ref[:, _C_KR:_C_END] = jnp.where(lane < QK_ROPE_DIM, kr, pltpu.roll(kr, QK_ROPE_DIM, 1)).astype(BF16)


def _prep_w_in(w):
    k, n = w.shape
    return pl.pallas_call(
        _prep_w_in_kernel,
        grid=(1,),
        in_specs=[_const_spec((n, k))],
        out_specs=pl.BlockSpec((k, _C_END), lambda i: (0, 0)),
        out_shape=jax.ShapeDtypeStruct((k, _C_END), BF16),
        compiler_params=_params("arbitrary"),
        name="prep_w_in",
    )(w.T)


def _rope_pairs(t, cos, sin, first_half):
    rot = jnp.where(first_half, -pltpu.roll(t, LANES - QK_ROPE_DIM // 2, 1),
                    pltpu.roll(t, QK_ROPE_DIM // 2, 1))
    return t * cos + rot * sin


def _in_conv_kernel(x_ref, mod_ref, gpre_ref, win_ref, qg_ref, wuqt_ref, kvg_ref, wuk_ref, wuvt_ref,
                    pos_ref, invf_ref, cw_ref, cb_ref, lng_ref, lnb_ref, wco_ref,
                    ya_ref, gate_ref, q_ref, k_ref, vt_ref, buf_ref, act_ref, *, q_scale, tm, nt, rows):
    t = pl.program_id(0) % nt

    @pl.when(t == 0)
    def _():
        buf_ref[0:CONV_HALO] = jnp.zeros((CONV_HALO, SUBLANES, LANES), F32)

    x = x_ref[...]
    shift = mod_ref[0, 0:1, :]
    scale = mod_ref[0, 1:2, :]
    h = (_rms(x) * gpre_ref[...]) * (1.0 + scale) + shift
    hb = h.astype(BF16)

    zg = _dot(hb, win_ref[:, _C_GLU:_C_GATE])
    u = zg[:, :CONV_DIM] * jax.nn.sigmoid(zg[:, CONV_DIM:])
    buf_ref[CONV_HALO:] = u.reshape(tm, SUBLANES, LANES)
    first = CONV_HALO - (CONV_KERNEL - 1)

    def conv_chunk(r0):
        acc = jnp.broadcast_to(cb_ref[...][None], (rows, SUBLANES, LANES))
        for k in range(CONV_KERNEL):
            acc = acc + cw_ref[k][None] * buf_ref[r0 + first + k:r0 + first + k + rows]
        acc = acc.reshape(rows, CONV_DIM)
        mu = jnp.mean(acc, axis=-1, keepdims=True)
        cen = acc - mu
        var = jnp.mean(cen * cen, axis=-1, keepdims=True)
        y = cen * lax.rsqrt(var + EPS) * lng_ref[...] + lnb_ref[...]
        y = y * jax.nn.sigmoid(y)
        act_ref[r0:r0 + rows, :] = y.astype(BF16)

    chunk_starts = list(range(0, tm, rows))
    per_stage = -(-len(chunk_starts) // 4)

    def conv_stage(i):
        for r0 in chunk_starts[i * per_stage:(i + 1) * per_stage]:
            conv_chunk(r0)

    gate_ref[...] = _dot(hb, win_ref[:, _C_GATE:_C_QLAT]).astype(BF16)
    conv_stage(0)

    ang_t = invf_ref[...] * pos_ref[0]
    cos_t = jnp.cos(ang_t)
    sin_t = jnp.sin(ang_t)
    reps = LANES // (QK_ROPE_DIM // 2)
    cos = jnp.concatenate([cos_t] * reps, axis=0).T
    sin = jnp.concatenate([sin_t] * reps, axis=0).T
    lane = lax.broadcasted_iota(jnp.int32, cos.shape, 1)
    first_half = (lane % QK_ROPE_DIM) < (QK_ROPE_DIM // 2)
    low = lane < QK_ROPE_DIM

    ql = _dot(hb, win_ref[:, _C_QLAT:_C_KVLAT])
    qn_t = (_rms(ql) * qg_ref[...]).T.astype(BF16)
    qt = _dot(wuqt_ref[...], qn_t) * q_scale
    nope_w = N_HEADS * QK_NOPE_DIM
    half_r = QK_ROPE_DIM // 2
    for hd in range(N_HEADS):
        q_ref[0, hd, 0:QK_NOPE_DIM, :] = qt[QK_NOPE_DIM * hd: QK_NOPE_DIM * (hd + 1)].astype(BF16)
        t1 = qt[nope_w + QK_ROPE_DIM * hd: nope_w + QK_ROPE_DIM * hd + half_r]
        t2 = qt[nope_w + QK_ROPE_DIM * hd + half_r: nope_w + QK_ROPE_DIM * (hd + 1)]
        q_ref[0, hd, QK_NOPE_DIM:QK_NOPE_DIM + half_r, :] = (t1 * cos_t - t2 * sin_t).astype(BF16)
        q_ref[0, hd, QK_NOPE_DIM + half_r:QK_NOPE_DIM + QK_ROPE_DIM, :] = (t2 * cos_t + t1 * sin_t).astype(BF16)
        q_ref[0, hd, QK_NOPE_DIM + QK_ROPE_DIM:, :] = jnp.zeros((QK_PAD_DIM - QK_NOPE_DIM - QK_ROPE_DIM, tm), BF16)
    conv_stage(1)

    kvl = _dot(hb, win_ref[:, _C_KVLAT:_C_KR])
    kvn_f = _rms(kvl) * kvg_ref[...]
    kvn = kvn_f.astype(BF16)
    kn = _dot(kvn, wuk_ref[...])
    kr = _rope_pairs(_dot(hb, win_ref[:, _C_KR:_C_END]), cos, sin, first_half)
    kr = jnp.where(low, kr, 0.0).astype(BF16)
    for hd in range(N_HEADS):
        k_ref[0, hd, :, 0:LANES] = kn[:, LANES * hd: LANES * (hd + 1)].astype(BF16)
        k_ref[0, hd, :, LANES:QK_PAD_DIM] = kr
    conv_stage(2)

    vt = _dot(wuvt_ref[...], kvn_f.T.astype(BF16))
    for hd in range(N_HEADS):
        vt_ref[0, hd] = vt[V_HEAD_DIM * hd: V_HEAD_DIM * (hd + 1), :].astype(BF16)
    conv_stage(3)

    ya_ref[...] = _dot(act_ref[...], wco_ref[...]).astype(BF16)
    buf_ref[0:CONV_HALO] = buf_ref[tm:tm + CONV_HALO]


def _in_conv(x2, mod3, g_pre, w_in_r, q_norm_g, w_uqt, kv_norm_g, w_uk, w_uvt, pos_f, inv_freq,
             conv_w8, conv_b8, ln_g, ln_b, w_conv_out, batch, seq, tm, rows):
    nt = seq // tm
    q_scale = (1.0 / math.sqrt(QK_NOPE_DIM + QK_ROPE_DIM)) * math.log2(math.e)
    tok = lambda i: (i, 0)
    heads = lambda i: (i // nt, 0, i % nt, 0)
    return pl.pallas_call(
        functools.partial(_in_conv_kernel, q_scale=q_scale, tm=tm, nt=nt, rows=rows),
        grid=(batch * nt,),
        in_specs=[
            pl.BlockSpec((tm, D_MODEL), tok),
            pl.BlockSpec((1, 6, D_MODEL), lambda i: (i // nt, 0, 0)),
            _const_spec((1, D_MODEL)),
            _const_spec(w_in_r.shape),
            _const_spec((1, Q_LORA_RANK)),
            _const_spec(w_uqt.shape),
            _const_spec((1, KV_LORA_RANK)),
            _const_spec(w_uk.shape),
            _const_spec(w_uvt.shape),
            pl.BlockSpec((1, 1, tm), lambda i: (i, 0, 0)),
            _const_spec((QK_ROPE_DIM // 2, 1)),
            _const_spec(conv_w8.shape),
            _const_spec((SUBLANES, LANES)),
            _const_spec((1, CONV_DIM)),
            _const_spec((1, CONV_DIM)),
            _const_spec(w_conv_out.shape),
        ],
        out_specs=[
            pl.BlockSpec((tm, D_MODEL), tok),
            pl.BlockSpec((tm, 2 * D_MODEL), tok),
            pl.BlockSpec((1, N_HEADS, QK_PAD_DIM, tm), lambda i: (i // nt, 0, 0, i % nt)),
            pl.BlockSpec((1, N_HEADS, tm, QK_PAD_DIM), heads),
            pl.BlockSpec((1, N_HEADS, V_HEAD_DIM, tm), lambda i: (i // nt, 0, 0, i % nt)),
        ],
        out_shape=[
            jax.ShapeDtypeStruct((batch * seq, D_MODEL), BF16),
            jax.ShapeDtypeStruct((batch * seq, 2 * D_MODEL), BF16),
            jax.ShapeDtypeStruct((batch, N_HEADS, QK_PAD_DIM, seq), BF16),
            jax.ShapeDtypeStruct((batch, N_HEADS, seq, QK_PAD_DIM), BF16),
            jax.ShapeDtypeStruct((batch, N_HEADS, V_HEAD_DIM, seq), BF16),
        ],
        scratch_shapes=[pltpu.VMEM((CONV_HALO + tm, SUBLANES, LANES), F32),
                        pltpu.VMEM((tm, CONV_DIM), BF16)],
        compiler_params=_params("arbitrary"),
        name="in_conv",
    )(x2, mod3, g_pre, w_in_r, q_norm_g, w_uqt, kv_norm_g, w_uk, w_uvt,
      pos_f.reshape(batch * nt, 1, tm), inv_freq,
      conv_w8, conv_b8, ln_g, ln_b, w_conv_out)


def _attn_kernel(qt_ref, k_ref, vt_ref, *rest, tq, tk, hpb, n_weights):
    w_refs, o_ref, wb_refs = rest[:n_weights], rest[n_weights], rest[n_weights + 1:2 * n_weights + 1]
    m_ref, acc_ref, s_ref = rest[2 * n_weights + 1:]
    for w_ref, wb_ref in zip(w_refs, wb_refs):
        wb_ref[...] = w_ref[...].astype(BF16)
    qi = pl.program_id(2)
    n_diag = tq // tk
    n_full = qi * n_diag
    m_ref[...] = jnp.full(m_ref.shape, -1e30, F32)
    acc_ref[...] = jnp.zeros(acc_ref.shape, F32)

    def tile_start(j):
        return pl.multiple_of(j * tk, tk)

    def scores(hd, j, q0=0):
        k = k_ref[0, hd, pl.ds(tile_start(j), tk), :]
        s_ref[hd, :, q0:] = _dot(k, qt_ref[0, hd, :, q0:])

    def update(hd, j, q0=0, diag=False):
        s = s_ref[hd, :, q0:]
        if diag:
            kpos = lax.broadcasted_iota(jnp.int32, (tk, tk), 0)
            qpos = lax.broadcasted_iota(jnp.int32, (tk, tk), 1)
            tri = jnp.where(qpos >= kpos, s[:, :tk], -1e30)
            s = tri if q0 + tk == tq else jnp.concatenate([tri, s[:, tk:]], axis=1)
        m_old = m_ref[hd, :, q0:]
        m_new = jnp.maximum(m_old, jnp.max(s, axis=0, keepdims=True))
        alpha = jnp.exp2(m_old - m_new)
        p = jnp.exp2(s - m_new)
        m_ref[hd, :, q0:] = m_new
        vt = jnp.concatenate([vt_ref[0, hd, :, pl.ds(tile_start(j), tk)], jnp.ones((DENOM_ROWS, tk), BF16)],
                             axis=0)
        acc_ref[hd, :, q0:] = alpha * acc_ref[hd, :, q0:] + _dot(vt, p.astype(BF16))

    scores(0, 0)

    def full_round(j, carry):
        for hd in range(hpb):
            if hd + 1 < hpb:
                scores(hd + 1, j)
            else:
                scores(0, j + 1)
            update(hd, j)
        return carry

    lax.fori_loop(0, n_full, full_round, 0)
    for d in range(n_diag):
        for hd in range(hpb):
            if hd + 1 < hpb:
                scores(hd + 1, n_full + d, d * tk)
            elif d + 1 < n_diag:
                scores(0, n_full + d + 1, (d + 1) * tk)
            update(hd, n_full + d, d * tk, True)
    for hd in range(hpb):
        o = acc_ref[hd, :V_HEAD_DIM, :] / acc_ref[hd, V_HEAD_DIM:V_HEAD_DIM + 1, :]
        o_ref[0, :, V_HEAD_DIM * hd: V_HEAD_DIM * (hd + 1)] = o.T.astype(BF16)


def _attn(qt4, k4, vt4, later_weights, tq, tk, hpb):
    batch, heads, seq, _ = k4.shape
    grid = (batch, heads // hpb, seq // tq)
    steps = grid[0] * grid[1] * grid[2]
    slab = lambda b, h, i: ((b * grid[1] + h) * grid[2] + i, 0)
    w_specs = [pl.BlockSpec((w.shape[0] // steps, w.shape[1]), slab) for w in later_weights]
    outs = pl.pallas_call(
        functools.partial(_attn_kernel, tq=tq, tk=tk, hpb=hpb,
                          n_weights=len(later_weights)),
        grid=grid,
        in_specs=[
            pl.BlockSpec((1, hpb, QK_PAD_DIM, tq), lambda b, h, i: (b, h, 0, i)),
            pl.BlockSpec((1, hpb, seq, QK_PAD_DIM), lambda b, h, i: (b, h, 0, 0)),
            pl.BlockSpec((1, hpb, V_HEAD_DIM, seq), lambda b, h, i: (b, h, 0, 0)),
        ] + w_specs,
        out_specs=[pl.BlockSpec((1, tq, hpb * V_HEAD_DIM), lambda b, h, i: (b, i, h))] + w_specs,
        out_shape=[jax.ShapeDtypeStruct((batch, seq, heads * V_HEAD_DIM), BF16)]
        + [jax.ShapeDtypeStruct(w.shape, BF16) for w in later_weights],
        scratch_shapes=[pltpu.VMEM((hpb, 1, tq), F32),
                        pltpu.VMEM((hpb, V_HEAD_DIM + DENOM_ROWS, tq), F32),
                        pltpu.VMEM((hpb, tk, tq), F32)],
        compiler_params=_params("arbitrary", "arbitrary", "arbitrary"),
        name="attn",
    )(qt4, k4, vt4, *later_weights)
    return outs[0], outs[1:]


def _mix_mlp_kernel(o_ref, ya_ref, gate_ref, x_ref, mod_ref, gpost_mix_ref, gpre_mlp_ref, gpost_mlp_ref,
                    wao_ref, wout_ref, w1_ref, w2_ref, out_ref):
    x = x_ref[...]
    y_b = _dot(o_ref[...], wao_ref[...])
    g_a = jax.nn.sigmoid(gate_ref[:, :D_MODEL].astype(F32))
    g_b = jax.nn.sigmoid(gate_ref[:, D_MODEL:].astype(F32))
    merged = g_a * ya_ref[...].astype(F32) + g_b * y_b
    y = _dot(merged.astype(BF16), wout_ref[...])
    x1 = x + mod_ref[0, 2:3, :] * (_rms(y) * gpost_mix_ref[...])

    shift = mod_ref[0, 3:4, :]
    scale = mod_ref[0, 4:5, :]
    h = (_rms(x1) * gpre_mlp_ref[...]) * (1.0 + scale) + shift
    hid = jnp.maximum(_dot(h.astype(BF16), w1_ref[...]), 0.0)
    y2 = _dot((hid * hid).astype(BF16), w2_ref[...])
    out_ref[...] = x1 + mod_ref[0, 5:6, :] * (_rms(y2) * gpost_mlp_ref[...])


def _mix_mlp(o2, ya2, gates, x2, mod3, g_post_mix, g_pre_mlp, g_post_mlp, w_attn_out, w_out, w1, w2, seq, tm):
    tokens = x2.shape[0]
    nt = seq // tm
    tok = lambda i: (i, 0)
    return pl.pallas_call(
        _mix_mlp_kernel,
        grid=(tokens // tm,),
        in_specs=[
            pl.BlockSpec((tm, D_MODEL), tok),
            pl.BlockSpec((tm, D_MODEL), tok),
            pl.BlockSpec((tm, 2 * D_MODEL), tok),
            pl.BlockSpec((tm, D_MODEL), tok),
            pl.BlockSpec((1, 6, D_MODEL), lambda i: (i // nt, 0, 0)),
            _const_spec((1, D_MODEL)),
            _const_spec((1, D_MODEL)),
            _const_spec((1, D_MODEL)),
            _const_spec(w_attn_out.shape),
            _const_spec(w_out.shape),
            _const_spec(w1.shape),
            _const_spec(w2.shape),
        ],
        out_specs=pl.BlockSpec((tm, D_MODEL), tok),
        out_shape=jax.ShapeDtypeStruct((tokens, D_MODEL), F32),
        compiler_params=_params("arbitrary"),
        name="mix_mlp",
    )(o2, ya2, gates, x2, mod3, g_post_mix, g_pre_mlp, g_post_mlp, w_attn_out, w_out, w1, w2)


def _head_major_cols(w, split):
    k = w.shape[0]
    w3 = w.reshape(k, N_HEADS, -1)
    return jnp.concatenate([w3[:, :, :split].reshape(k, -1), w3[:, :, split:].reshape(k, -1)], axis=1)


def kernel(x, c, positions, w_ada, b_ada, g_pre_mix, g_post_mix, g_pre_mlp, g_post_mlp, w_in, conv_w, conv_b, conv_norm_g, conv_norm_b, w_conv_out, q_norm_g, w_uq, kv_norm_g, w_ukv, w_attn_out, w_out, w_mlp_in, w_mlp_out):
    batch, seq, d = x.shape
    depth = w_ada.shape[0]
    tokens = batch * seq
    row = lambda v: v.reshape(1, -1)

    inv_freq = 1.0 / (ROPE_THETA ** (jnp.arange(0, QK_ROPE_DIM, 2, dtype=F32) / QK_ROPE_DIM))
    inv_freq = inv_freq.reshape(-1, 1)
    pos_f = positions.astype(F32)
    c_pad = jnp.pad(c, ((0, 8 - batch), (0, 0)))

    x2 = x.reshape(tokens, d)
    for l in range(depth):
        w_in_r = _prep_w_in(w_in[l])
        w_uqt = _head_major_cols(w_uq[l], QK_NOPE_DIM).astype(BF16).T
        w_ukv_r = _head_major_cols(w_ukv[l], QK_NOPE_DIM).astype(BF16)
        w_uk = w_ukv_r[:, :N_HEADS * QK_NOPE_DIM]
        w_uvt = w_ukv_r[:, N_HEADS * QK_NOPE_DIM:].T

        mod = _ada(c_pad, w_ada[l], row(b_ada[l]))
        mod3 = mod[:batch].reshape(batch, 6, d)

        conv_w8 = conv_w[l].reshape(CONV_KERNEL, SUBLANES, LANES)
        conv_b8 = conv_b[l].reshape(SUBLANES, LANES)
        y_a, gates, q4, k4, vt4 = _in_conv(x2, mod3, row(g_pre_mix[l]), w_in_r, row(q_norm_g[l]), w_uqt,
                                           row(kv_norm_g[l]), w_uk, w_uvt, pos_f, inv_freq, conv_w8, conv_b8,
                                           row(conv_norm_g[l]), row(conv_norm_b[l]), w_conv_out[l].astype(BF16),
                                           batch, seq, tm=IN_CONV_ROWS, rows=CONV_CHUNK_ROWS)
        o, (w_ao, w_o, w_1, w_2) = _attn(q4, k4, vt4, (w_attn_out[l], w_out[l], w_mlp_in[l], w_mlp_out[l]),
                                         tq=ATTN_Q_ROWS, tk=ATTN_K_ROWS, hpb=ATTN_HEADS_PER_STEP)
        x2 = _mix_mlp(o.reshape(tokens, d), y_a, gates, x2, mod3, row(g_post_mix[l]), row(g_pre_mlp[l]),
                      row(g_post_mlp[l]), w_ao, w_o, w_1, w_2, seq, tm=MIX_MLP_ROWS)
    return x2.reshape(batch, seq, d)
```

```python
import functools
import math

import jax
import jax.numpy as jnp
from jax import lax
from jax.experimental import pallas as pl
from jax.experimental.pallas import tpu as pltpu

D_MODEL = 1024
CONV_DIM = 1024
CONV_KERNEL = 31
N_HEADS = 8
QK_NOPE_DIM = 128
QK_ROPE_DIM = 64
V_HEAD_DIM = 128
Q_LORA_RANK = 384
KV_LORA_RANK = 256
ROPE_THETA = 10000.0
EPS = 1e-6

LANES = 128
SUBLANES = 8
QK_PAD_DIM = 2 * LANES
DENOM_ROWS = 16
CONV_HALO = 32
VMEM_LIMIT = 56 * 1024 * 1024

IN_CONV_ROWS = 512
CONV_CHUNK_ROWS = 32
ATTN_Q_ROWS = 1024
ATTN_K_ROWS = 512
ATTN_HEADS_PER_STEP = 4
MIX_MLP_ROWS = 512

_C_GLU = 0
_C_GATE = 2 * CONV_DIM
_C_QLAT = _C_GATE + 2 * D_MODEL
_C_KVLAT = _C_QLAT + Q_LORA_RANK
_C_KR = _C_KVLAT + KV_LORA_RANK
_C_END = _C_KR + LANES

BF16 = jnp.bfloat16
F32 = jnp.float32


def _dot(a, b):
    return jnp.dot(a, b, preferred_element_type=F32)


def _rms(v):
    return v * lax.rsqrt(jnp.mean(v * v, axis=-1, keepdims=True) + EPS)


def _params(*sem):
    return pltpu.CompilerParams(dimension_semantics=sem, vmem_limit_bytes=VMEM_LIMIT)


def _const_spec(shape):
    nd = len(shape)
    return pl.BlockSpec(shape, lambda *_: (0,) * nd, pipeline_mode=pl.Buffered(1))


def _ada_kernel(c_ref, w_ref, b_ref, o_ref):
    c = c_ref[...]
    act = c * jax.nn.sigmoid(c)
    o_ref[...] = _dot(act.astype(BF16), w_ref[...].astype(BF16)) + b_ref[...]


def _ada(c_pad, w_ada, b_ada):
    rows, d = c_pad.shape
    n = w_ada.shape[1]
    tn = 1024
    return pl.pallas_call(
        _ada_kernel,
        grid=(n // tn,),
        in_specs=[
            pl.BlockSpec((rows, d), lambda j: (0, 0)),
            pl.BlockSpec((d, tn), lambda j: (0, j)),
            pl.BlockSpec((1, tn), lambda j: (0, j)),
        ],
        out_specs=pl.BlockSpec((rows, tn), lambda j: (0, j)),
        out_shape=jax.ShapeDtypeStruct((rows, n), F32),
        compiler_params=_params("arbitrary"),
        name="ada",
    )(c_pad, w_ada, b_ada)


_S_QLAT = 2 * CONV_DIM
_S_KVLAT = _S_QLAT + Q_LORA_RANK
_S_KR = _S_KVLAT + KV_LORA_RANK
_S_GATE = _S_KR + QK_ROPE_DIM


def _prep_w_in_kernel(wt_ref, o_ref):
    def move(dst, src, n):
        for off in range(0, n, QK_PAD_DIM):
            m = min(QK_PAD_DIM, n - off)
            o_ref[:, dst + off:dst + off + m] = wt_ref[src + off:src + off + m, :].T.astype(BF16)

    move(_C_GLU, 0, _S_QLAT)
    move(_C_GATE, _S_GATE, _C_QLAT - _C_GATE)
    move(_C_QLAT, _S_QLAT, Q_LORA_RANK)
    move(_C_KVLAT, _S_KVLAT, KV_LORA_RANK)
    kr = wt_ref[_S_KR:_S_KR + LANES, :].T
    lane = lax.broadcasted_iota(jnp.int32, kr.shape, 1)
    o_ref[:, _C_KR:_C_END] = jnp.where(lane < QK_ROPE_DIM, kr, pltpu.roll(kr, QK_ROPE_DIM, 1)).astype(BF16)


def _prep_w_in(w):
    k, n = w.shape
    return pl.pallas_call(
        _prep_w_in_kernel,
        grid=(1,),
        in_specs=[_const_spec((n, k))],
        out_specs=pl.BlockSpec((k, _C_END), lambda i: (0, 0)),
        out_shape=jax.ShapeDtypeStruct((k, _C_END), BF16),
        compiler_params=_params("arbitrary"),
        name="prep_w_in",
    )(w.T)


def _rope_pairs(t, cos, sin, first_half):
    rot = jnp.where(first_half, -pltpu.roll(t, LANES - QK_ROPE_DIM // 2, 1),
                    pltpu.roll(t, QK_ROPE_DIM // 2, 1))
    return t * cos + rot * sin


def _in_conv_kernel(x_ref, mod_ref, gpre_ref, win_ref, qg_ref, wuqt_ref, kvg_ref, wuk_ref, wuvt_ref,
                    pos_ref, invf_ref, cw_ref, cb_ref, lng_ref, lnb_ref, wco_ref,
                    ya_ref, gate_ref, q_ref, k_ref, vt_ref, buf_ref, act_ref, *, q_scale, tm, nt, rows):
    t = pl.program_id(0) % nt

    @pl.when(t == 0)
    def _():
        buf_ref[0:CONV_HALO] = jnp.zeros((CONV_HALO, SUBLANES, LANES), F32)

    x = x_ref[...]
    shift = mod_ref[0, 0:1, :]
    scale = mod_ref[0, 1:2, :]
    h = (_rms(x) * gpre_ref[...]) * (1.0 + scale) + shift
    hb = h.astype(BF16)

    zg = _dot(hb, win_ref[:, _C_GLU:_C_GATE])
    u = zg[:, :CONV_DIM] * jax.nn.sigmoid(zg[:, CONV_DIM:])
    buf_ref[CONV_HALO:] = u.reshape(tm, SUBLANES, LANES)
    first = CONV_HALO - (CONV_KERNEL - 1)

    def conv_chunk(r0):
        acc = jnp.broadcast_to(cb_ref[...][None], (rows, SUBLANES, LANES))
        for k in range(CONV_KERNEL):
            acc = acc + cw_ref[k][None] * buf_ref[r0 + first + k:r0 + first + k + rows]
        acc = acc.reshape(rows, CONV_DIM)
        mu = jnp.mean(acc, axis=-1, keepdims=True)
        cen = acc - mu
        var = jnp.mean(cen * cen, axis=-1, keepdims=True)
        y = cen * lax.rsqrt(var + EPS) * lng_ref[...] + lnb_ref[...]
        y = y * jax.nn.sigmoid(y)
        act_ref[r0:r0 + rows, :] = y.astype(BF16)

    chunk_starts = list(range(0, tm, rows))
    per_stage = -(-len(chunk_starts) // 4)

    def conv_stage(i):
        for r0 in chunk_starts[i * per_stage:(i + 1) * per_stage]:
            conv_chunk(r0)

    gate_ref[...] = _dot(hb, win_ref[:, _C_GATE:_C_QLAT]).astype(BF16)
    conv_stage(0)

    ang_t = invf_ref[...] * pos_ref[0]
    cos_t = jnp.cos(ang_t)
    sin_t = jnp.sin(ang_t)
    reps = LANES // (QK_ROPE_DIM // 2)
    cos = jnp.concatenate([cos_t] * reps, axis=0).T
    sin = jnp.concatenate([sin_t] * reps, axis=0).T
    lane = lax.broadcasted_iota(jnp.int32, cos.shape, 1)
    first_half = (lane % QK_ROPE_DIM) < (QK_ROPE_DIM // 2)
    low = lane < QK_ROPE_DIM

    ql = _dot(hb, win_ref[:, _C_QLAT:_C_KVLAT])
    qn = (_rms(ql) * qg_ref[...]).astype(BF16)
    qt = lax.dot_general(wuqt_ref[...], qn, (((1,), (1,)), ((), ())), preferred_element_type=F32) * q_scale
    nope_w = N_HEADS * QK_NOPE_DIM
    half_r = QK_ROPE_DIM // 2
    for hd in range(N_HEADS):
        q_ref[0, hd, 0:QK_NOPE_DIM, :] = qt[QK_NOPE_DIM * hd: QK_NOPE_DIM * (hd + 1)].astype(BF16)
        t1 = qt[nope_w + QK_ROPE_DIM * hd: nope_w + QK_ROPE_DIM * hd + half_r]
        t2 = qt[nope_w + QK_ROPE_DIM * hd + half_r: nope_w + QK_ROPE_DIM * (hd + 1)]
        q_ref[0, hd, QK_NOPE_DIM:QK_NOPE_DIM + half_r, :] = (t1 * cos_t - t2 * sin_t).astype(BF16)
        q_ref[0, hd, QK_NOPE_DIM + half_r:QK_NOPE_DIM + QK_ROPE_DIM, :] = (t2 * cos_t + t1 * sin_t).astype(BF16)
        q_ref[0, hd, QK_NOPE_DIM + QK_ROPE_DIM:, :] = jnp.zeros((QK_PAD_DIM - QK_NOPE_DIM - QK_ROPE_DIM, tm), BF16)
    conv_stage(1)

    kvl = _dot(hb, win_ref[:, _C_KVLAT:_C_KR])
    kvn = (_rms(kvl) * kvg_ref[...]).astype(BF16)
    kn = _dot(kvn, wuk_ref[...])
    kr = _rope_pairs(_dot(hb, win_ref[:, _C_KR:_C_END]), cos, sin, first_half)
    kr = jnp.where(low, kr, 0.0).astype(BF16)
    for hd in range(N_HEADS):
        k_ref[0, hd, :, 0:LANES] = kn[:, LANES * hd: LANES * (hd + 1)].astype(BF16)
        k_ref[0, hd, :, LANES:QK_PAD_DIM] = kr
    conv_stage(2)

    vt = lax.dot_general(wuvt_ref[...], kvn, (((1,), (1,)), ((), ())), preferred_element_type=F32)
    for hd in range(N_HEADS):
        vt_ref[0, hd] = vt[V_HEAD_DIM * hd: V_HEAD_DIM * (hd + 1), :].astype(BF16)
    conv_stage(3)

    ya_ref[...] = _dot(act_ref[...], wco_ref[...]).astype(BF16)
    buf_ref[0:CONV_HALO] = buf_ref[tm:tm + CONV_HALO]


def _in_conv(x2, mod3, g_pre, w_in_r, q_norm_g, w_uqt, kv_norm_g, w_uk, w_uvt, pos_f, inv_freq,
             conv_w8, conv_b8, ln_g, ln_b, w_conv_out, batch, seq, tm, rows):
    nt = seq // tm
    q_scale = (1.0 / math.sqrt(QK_NOPE_DIM + QK_ROPE_DIM)) * math.log2(math.e)
    tok = lambda i: (i, 0)
    heads = lambda i: (i // nt, 0, i % nt, 0)
    return pl.pallas_call(
        functools.partial(_in_conv_kernel, q_scale=q_scale, tm=tm, nt=nt, rows=rows),
        grid=(batch * nt,),
        in_specs=[
            pl.BlockSpec((tm, D_MODEL), tok),
            pl.BlockSpec((1, 6, D_MODEL), lambda i: (i // nt, 0, 0)),
            _const_spec((1, D_MODEL)),
            _const_spec(w_in_r.shape),
            _const_spec((1, Q_LORA_RANK)),
            _const_spec(w_uqt.shape),
            _const_spec((1, KV_LORA_RANK)),
            _const_spec(w_uk.shape),
            _const_spec(w_uvt.shape),
            pl.BlockSpec((1, 1, tm), lambda i: (i, 0, 0)),
            _const_spec((QK_ROPE_DIM // 2, 1)),
            _const_spec(conv_w8.shape),
            _const_spec((SUBLANES, LANES)),
            _const_spec((1, CONV_DIM)),
            _const_spec((1, CONV_DIM)),
            _const_spec(w_conv_out.shape),
        ],
        out_specs=[
            pl.BlockSpec((tm, D_MODEL), tok),
            pl.BlockSpec((tm, 2 * D_MODEL), tok),
            pl.BlockSpec((1, N_HEADS, QK_PAD_DIM, tm), lambda i: (i // nt, 0, 0, i % nt)),
            pl.BlockSpec((1, N_HEADS, tm, QK_PAD_DIM), heads),
            pl.BlockSpec((1, N_HEADS, V_HEAD_DIM, tm), lambda i: (i // nt, 0, 0, i % nt)),
        ],
        out_shape=[
            jax.ShapeDtypeStruct((batch * seq, D_MODEL), BF16),
            jax.ShapeDtypeStruct((batch * seq, 2 * D_MODEL), BF16),
            jax.ShapeDtypeStruct((batch, N_HEADS, QK_PAD_DIM, seq), BF16),
            jax.ShapeDtypeStruct((batch, N_HEADS, seq, QK_PAD_DIM), BF16),
            jax.ShapeDtypeStruct((batch, N_HEADS, V_HEAD_DIM, seq), BF16),
        ],
        scratch_shapes=[pltpu.VMEM((CONV_HALO + tm, SUBLANES, LANES), F32),
                        pltpu.VMEM((tm, CONV_DIM), BF16)],
        compiler_params=_params("arbitrary"),
        name="in_conv",
    )(x2, mod3, g_pre, w_in_r, q_norm_g, w_uqt, kv_norm_g, w_uk, w_uvt,
      pos_f.reshape(batch * nt, 1, tm), inv_freq,
      conv_w8, conv_b8, ln_g, ln_b, w_conv_out)


def _attn_kernel(qt_ref, k_ref, vt_ref, *rest, tq, tk, hpb, n_weights):
    w_refs, o_ref, wb_refs = rest[:n_weights], rest[n_weights], rest[n_weights + 1:2 * n_weights + 1]
    m_ref, acc_ref, s_ref = rest[2 * n_weights + 1:]
    for w_ref, wb_ref in zip(w_refs, wb_refs):
        wb_ref[...] = w_ref[...].astype(BF16)
    qi = pl.program_id(2)
    n_diag = tq // tk
    n_full = qi * n_diag
    m_ref[...] = jnp.full(m_ref.shape, -1e30, F32)
    acc_ref[...] = jnp.zeros(acc_ref.shape, F32)

    def tile_start(j):
        return pl.multiple_of(j * tk, tk)

    def scores(hd, j, q0=0):
        k = k_ref[0, hd, pl.ds(tile_start(j), tk), :]
        s_ref[hd, :, q0:] = _dot(k, qt_ref[0, hd, :, q0:])

    def update(hd, j, q0=0, diag=False):
        s = s_ref[hd, :, q0:]
        if diag:
            kpos = lax.broadcasted_iota(jnp.int32, (tk, tk), 0)
            qpos = lax.broadcasted_iota(jnp.int32, (tk, tk), 1)
            tri = jnp.where(qpos >= kpos, s[:, :tk], -1e30)
            s = tri if q0 + tk == tq else jnp.concatenate([tri, s[:, tk:]], axis=1)
        m_old = m_ref[hd, :, q0:]
        m_new = jnp.maximum(m_old, jnp.max(s, axis=0, keepdims=True))
        alpha = jnp.exp2(m_old - m_new)
        p = jnp.exp2(s - m_new)
        m_ref[hd, :, q0:] = m_new
        vt = jnp.concatenate([vt_ref[0, hd, :, pl.ds(tile_start(j), tk)], jnp.ones((DENOM_ROWS, tk), BF16)],
                             axis=0)
        acc_ref[hd, :, q0:] = alpha * acc_ref[hd, :, q0:] + _dot(vt, p.astype(BF16))

    scores(0, 0)

    def full_round(j, carry):
        for hd in range(hpb):
            if hd + 1 < hpb:
                scores(hd + 1, j)
            else:
                scores(0, j + 1)
            update(hd, j)
        return carry

    lax.fori_loop(0, n_full, full_round, 0)
    for d in range(n_diag):
        for hd in range(hpb):
            if hd + 1 < hpb:
                scores(hd + 1, n_full + d, d * tk)
            elif d + 1 < n_diag:
                scores(0, n_full + d + 1, (d + 1) * tk)
            update(hd, n_full + d, d * tk, True)
    for hd in range(hpb):
        o = acc_ref[hd, :V_HEAD_DIM, :] / acc_ref[hd, V_HEAD_DIM:V_HEAD_DIM + 1, :]
        o_ref[0, :, V_HEAD_DIM * hd: V_HEAD_DIM * (hd + 1)] = o.T.astype(BF16)


def _attn(qt4, k4, vt4, later_weights, tq, tk, hpb):
    batch, heads, seq, _ = k4.shape
    grid = (batch, heads // hpb, seq // tq)
    steps = grid[0] * grid[1] * grid[2]
    slab = lambda b, h, i: ((b * grid[1] + h) * grid[2] + i, 0)
    w_specs = [pl.BlockSpec((w.shape[0] // steps, w.shape[1]), slab) for w in later_weights]
    outs = pl.pallas_call(
        functools.partial(_attn_kernel, tq=tq, tk=tk, hpb=hpb,
                          n_weights=len(later_weights)),
        grid=grid,
        in_specs=[
            pl.BlockSpec((1, hpb, QK_PAD_DIM, tq), lambda b, h, i: (b, h, 0, i)),
            pl.BlockSpec((1, hpb, seq, QK_PAD_DIM), lambda b, h, i: (b, h, 0, 0)),
            pl.BlockSpec((1, hpb, V_HEAD_DIM, seq), lambda b, h, i: (b, h, 0, 0)),
        ] + w_specs,
        out_specs=[pl.BlockSpec((1, tq, hpb * V_HEAD_DIM), lambda b, h, i: (b, i, h))] + w_specs,
        out_shape=[jax.ShapeDtypeStruct((batch, seq, heads * V_HEAD_DIM), BF16)]
        + [jax.ShapeDtypeStruct(w.shape, BF16) for w in later_weights],
        scratch_shapes=[pltpu.VMEM((hpb, 1, tq), F32),
                        pltpu.VMEM((hpb, V_HEAD_DIM + DENOM_ROWS, tq), F32),
                        pltpu.VMEM((hpb, tk, tq), F32)],
        compiler_params=_params("arbitrary", "arbitrary", "arbitrary"),
        name="attn",
    )(qt4, k4, vt4, *later_weights)
    return outs[0], outs[1:]


def _mix_mlp_kernel(o_ref, ya_ref, gate_ref, x_ref, mod_ref, gpost_mix_ref, gpre_mlp_ref, gpost_mlp_ref,
                    wao_ref, wout_ref, w1_ref, w2_ref, out_ref):
    x = x_ref[...]
    y_b = _dot(o_ref[...], wao_ref[...])
    g_a = jax.nn.sigmoid(gate_ref[:, :D_MODEL].astype(F32))
    g_b = jax.nn.sigmoid(gate_ref[:, D_MODEL:].astype(F32))
    merged = g_a * ya_ref[...].astype(F32) + g_b * y_b
    y = _dot(merged.astype(BF16), wout_ref[...])
    x1 = x + mod_ref[0, 2:3, :] * (_rms(y) * gpost_mix_ref[...])

    shift = mod_ref[0, 3:4, :]
    scale = mod_ref[0, 4:5, :]
    h = (_rms(x1) * gpre_mlp_ref[...]) * (1.0 + scale) + shift
    hid = jnp.maximum(_dot(h.astype(BF16), w1_ref[...]), 0.0)
    y2 = _dot((hid * hid).astype(BF16), w2_ref[...])
    out_ref[...] = x1 + mod_ref[0, 5:6, :] * (_rms(y2) * gpost_mlp_ref[...])


def _mix_mlp(o2, ya2, gates, x2, mod3, g_post_mix, g_pre_mlp, g_post_mlp, w_attn_out, w_out, w1, w2, seq, tm):
    tokens = x2.shape[0]
    nt = seq // tm
    tok = lambda i: (i, 0)
    return pl.pallas_call(
        _mix_mlp_kernel,
        grid=(tokens // tm,),
        in_specs=[
            pl.BlockSpec((tm, D_MODEL), tok),
            pl.BlockSpec((tm, D_MODEL), tok),
            pl.BlockSpec((tm, 2 * D_MODEL), tok),
            pl.BlockSpec((tm, D_MODEL), tok),
            pl.BlockSpec((1, 6, D_MODEL), lambda i: (i // nt, 0, 0)),
            _const_spec((1, D_MODEL)),
            _const_spec((1, D_MODEL)),
            _const_spec((1, D_MODEL)),
            _const_spec(w_attn_out.shape),
            _const_spec(w_out.shape),
            _const_spec(w1.shape),
            _const_spec(w2.shape),
        ],
        out_specs=pl.BlockSpec((tm, D_MODEL), tok),
        out_shape=jax.ShapeDtypeStruct((tokens, D_MODEL), F32),
        compiler_params=_params("arbitrary"),
        name="mix_mlp",
    )(o2, ya2, gates, x2, mod3, g_post_mix, g_pre_mlp, g_post_mlp, w_attn_out, w_out, w1, w2)


def _head_major_cols(w, split):
    k = w.shape[0]
    w3 = w.reshape(k, N_HEADS, -1)
    return jnp.concatenate([w3[:, :, :split].reshape(k, -1), w3[:, :, split:].reshape(k, -1)], axis=1)


def kernel(x, c, positions, w_ada, b_ada, g_pre_mix, g_post_mix, g_pre_mlp, g_post_mlp, w_in, conv_w, conv_b, conv_norm_g, conv_norm_b, w_conv_out, q_norm_g, w_uq, kv_norm_g, w_ukv, w_attn_out, w_out, w_mlp_in, w_mlp_out):
    batch, seq, d = x.shape
    depth = w_ada.shape[0]
    tokens = batch * seq
    row = lambda v: v.reshape(1, -1)

    inv_freq = 1.0 / (ROPE_THETA ** (jnp.arange(0, QK_ROPE_DIM, 2, dtype=F32) / QK_ROPE_DIM))
    inv_freq = inv_freq.reshape(-1, 1)
    pos_f = positions.astype(F32)
    c_pad = jnp.pad(c, ((0, 8 - batch), (0, 0)))

    x2 = x.reshape(tokens, d)
    for l in range(depth):
        w_in_r = _prep_w_in(w_in[l])
        w_uqt = _head_major_cols(w_uq[l], QK_NOPE_DIM).astype(BF16).T
        w_ukv_r = _head_major_cols(w_ukv[l], QK_NOPE_DIM).astype(BF16)
        w_uk = w_ukv_r[:, :N_HEADS * QK_NOPE_DIM]
        w_uvt = w_ukv_r[:, N_HEADS * QK_NOPE_DIM:].T

        mod = _ada(c_pad, w_ada[l], row(b_ada[l]))
        mod3 = mod[:batch].reshape(batch, 6, d)

        conv_w8 = conv_w[l].reshape(CONV_KERNEL, SUBLANES, LANES)
        conv_b8 = conv_b[l].reshape(SUBLANES, LANES)
        y_a, gates, q4, k4, vt4 = _in_conv(x2, mod3, row(g_pre_mix[l]), w_in_r, row(q_norm_g[l]), w_uqt,
                                           row(kv_norm_g[l]), w_uk, w_uvt, pos_f, inv_freq, conv_w8, conv_b8,
                                           row(conv_norm_g[l]), row(conv_norm_b[l]), w_conv_out[l].astype(BF16),
                                           batch, seq, tm=IN_CONV_ROWS, rows=CONV_CHUNK_ROWS)
        o, (w_ao, w_o, w_1, w_2) = _attn(q4, k4, vt4, (w_attn_out[l], w_out[l], w_mlp_in[l], w_mlp_out[l]),
                                         tq=ATTN_Q_ROWS, tk=ATTN_K_ROWS, hpb=ATTN_HEADS_PER_STEP)
        x2 = _mix_mlp(o.reshape(tokens, d), y_a, gates, x2, mod3, row(g_post_mix[l]), row(g_pre_mlp[l]),
                      row(g_post_mlp[l]), w_ao, w_o, w_1, w_2, seq, tm=MIX_MLP_ROWS)
    return x2.reshape(batch, seq, d)
```

```python
import functools
import math

import jax
import jax.numpy as jnp
from jax import lax
from jax.experimental import pallas as pl
from jax.experimental.pallas import tpu as pltpu

D_MODEL = 1024
CONV_DIM = 1024
CONV_KERNEL = 31
N_HEADS = 8
QK_NOPE_DIM = 128
QK_ROPE_DIM = 64
V_HEAD_DIM = 128
Q_LORA_RANK = 384
KV_LORA_RANK = 256
ROPE_THETA = 10000.0
EPS = 1e-6

LANES = 128
SUBLANES = 8
QK_PAD_DIM = 2 * LANES
DENOM_ROWS = 16
CONV_HALO = 32
VMEM_LIMIT = 56 * 1024 * 1024

IN_CONV_ROWS = 512
CONV_CHUNK_ROWS = 32
ATTN_Q_ROWS = 1024
ATTN_K_ROWS = 512
ATTN_HEADS_PER_STEP = 4
MIX_MLP_ROWS = 512

_C_GLU = 0
_C_GATE = 2 * CONV_DIM
_C_QLAT = _C_GATE + 2 * D_MODEL
_C_KVLAT = _C_QLAT + Q_LORA_RANK
_C_KR = _C_KVLAT + KV_LORA_RANK
_C_END = _C_KR + LANES

BF16 = jnp.bfloat16
F32 = jnp.float32


def _dot(a, b):
    return jnp.dot(a, b, preferred_element_type=F32)


def _rms(v):
    return v * lax.rsqrt(jnp.mean(v * v, axis=-1, keepdims=True) + EPS)


def _params(*sem):
    return pltpu.CompilerParams(dimension_semantics=sem, vmem_limit_bytes=VMEM_LIMIT)


def _const_spec(shape):
    nd = len(shape)
    return pl.BlockSpec(shape, lambda *_: (0,) * nd, pipeline_mode=pl.Buffered(1))


def _ada_kernel(c_ref, w_ref, b_ref, o_ref):
    c = c_ref[...]
    act = c * jax.nn.sigmoid(c)
    o_ref[...] = _dot(act.astype(BF16), w_ref[...].astype(BF16)) + b_ref[...]


def _ada(c_pad, w_ada, b_ada):
    rows, d = c_pad.shape
    n = w_ada.shape[1]
    tn = 1024
    return pl.pallas_call(
        _ada_kernel,
        grid=(n // tn,),
        in_specs=[
            pl.BlockSpec((rows, d), lambda j: (0, 0)),
            pl.BlockSpec((d, tn), lambda j: (0, j)),
            pl.BlockSpec((1, tn), lambda j: (0, j)),
        ],
        out_specs=pl.BlockSpec((rows, tn), lambda j: (0, j)),
        out_shape=jax.ShapeDtypeStruct((rows, n), F32),
        compiler_params=_params("arbitrary"),
        name="ada",
    )(c_pad, w_ada, b_ada)


_S_QLAT = 2 * CONV_DIM
_S_KVLAT = _S_QLAT + Q_LORA_RANK
_S_KR = _S_KVLAT + KV_LORA_RANK
_S_GATE = _S_KR + QK_ROPE_DIM


def _prep_w_in_kernel(wt_ref, o_ref):
    def move(dst, src, n):
        for off in range(0, n, QK_PAD_DIM):
            m = min(QK_PAD_DIM, n - off)
            o_ref[:, dst + off:dst + off + m] = wt_ref[src + off:src + off + m, :].T.astype(BF16)

    move(_C_GLU, 0, _S_QLAT)
    move(_C_GATE, _S_GATE, _C_QLAT - _C_GATE)
    move(_C_QLAT, _S_QLAT, Q_LORA_RANK)
    move(_C_KVLAT, _S_KVLAT, KV_LORA_RANK)
    kr = wt_ref[_S_KR:_S_KR + LANES, :].T
    lane = lax.broadcasted_iota(jnp.int32, kr.shape, 1)
    o_ref[:, _C_KR:_C_END] = jnp.where(lane < QK_ROPE_DIM, kr, pltpu.roll(kr, QK_ROPE_DIM, 1)).astype(BF16)


def _prep_w_in(w):
    k, n = w.shape
    tr = QK_PAD_DIM
    return pl.pallas_call(
        _prep_w_in_kernel,
        grid=(k // tr,),
        in_specs=[pl.BlockSpec((n, tr), lambda i: (0, i))],
        out_specs=pl.BlockSpec((tr, _C_END), lambda i: (i, 0)),
        out_shape=jax.ShapeDtypeStruct((k, _C_END), BF16),
        compiler_params=_params("arbitrary"),
        name="prep_w_in",
    )(w.T)


def _rope_pairs(t, cos, sin, first_half):
    rot = jnp.where(first_half, -pltpu.roll(t, LANES - QK_ROPE_DIM // 2, 1),
                    pltpu.roll(t, QK_ROPE_DIM // 2, 1))
    return t * cos + rot * sin


def _in_conv_kernel(x_ref, mod_ref, gpre_ref, win_ref, qg_ref, wuqt_ref, kvg_ref, wuk_ref, wuvt_ref,
                    pos_ref, invf_ref, cw_ref, cb_ref, lng_ref, lnb_ref, wco_ref,
                    ya_ref, gate_ref, q_ref, k_ref, vt_ref, buf_ref, act_ref, *, q_scale, tm, nt, rows):
    t = pl.program_id(0) % nt

    @pl.when(t == 0)
    def _():
        buf_ref[0:CONV_HALO] = jnp.zeros((CONV_HALO, SUBLANES, LANES), F32)

    x = x_ref[...]
    shift = mod_ref[0, 0:1, :]
    scale = mod_ref[0, 1:2, :]
    h = (_rms(x) * gpre_ref[...]) * (1.0 + scale) + shift
    hb = h.astype(BF16)

    zg = _dot(hb, win_ref[:, _C_GLU:_C_GATE])
    u = zg[:, :CONV_DIM] * jax.nn.sigmoid(zg[:, CONV_DIM:])
    buf_ref[CONV_HALO:] = u.reshape(tm, SUBLANES, LANES)
    first = CONV_HALO - (CONV_KERNEL - 1)

    def conv_chunk(r0):
        acc = jnp.broadcast_to(cb_ref[...][None], (rows, SUBLANES, LANES))
        for k in range(CONV_KERNEL):
            acc = acc + cw_ref[k][None] * buf_ref[r0 + first + k:r0 + first + k + rows]
        acc = acc.reshape(rows, CONV_DIM)
        mu = jnp.mean(acc, axis=-1, keepdims=True)
        cen = acc - mu
        var = jnp.mean(cen * cen, axis=-1, keepdims=True)
        y = cen * lax.rsqrt(var + EPS) * lng_ref[...] + lnb_ref[...]
        y = y * jax.nn.sigmoid(y)
        act_ref[r0:r0 + rows, :] = y.astype(BF16)

    chunk_starts = list(range(0, tm, rows))
    per_stage = -(-len(chunk_starts) // 4)

    def conv_stage(i):
        for r0 in chunk_starts[i * per_stage:(i + 1) * per_stage]:
            conv_chunk(r0)

    gate_ref[...] = _dot(hb, win_ref[:, _C_GATE:_C_QLAT]).astype(BF16)
    conv_stage(0)

    ang_t = invf_ref[...] * pos_ref[0]
    cos_t = jnp.cos(ang_t)
    sin_t = jnp.sin(ang_t)
    reps = LANES // (QK_ROPE_DIM // 2)
    cos = jnp.concatenate([cos_t] * reps, axis=0).T
    sin = jnp.concatenate([sin_t] * reps, axis=0).T
    lane = lax.broadcasted_iota(jnp.int32, cos.shape, 1)
    first_half = (lane % QK_ROPE_DIM) < (QK_ROPE_DIM // 2)
    low = lane < QK_ROPE_DIM

    ql = _dot(hb, win_ref[:, _C_QLAT:_C_KVLAT])
    qn = (_rms(ql) * qg_ref[...]).astype(BF16)
    qt = lax.dot_general(wuqt_ref[...], qn, (((1,), (1,)), ((), ())), preferred_element_type=F32) * q_scale
    nope_w = N_HEADS * QK_NOPE_DIM
    half_r = QK_ROPE_DIM // 2
    for hd in range(N_HEADS):
        q_ref[0, hd, 0:QK_NOPE_DIM, :] = qt[QK_NOPE_DIM * hd: QK_NOPE_DIM * (hd + 1)].astype(BF16)
        t1 = qt[nope_w + QK_ROPE_DIM * hd: nope_w + QK_ROPE_DIM * hd + half_r]
        t2 = qt[nope_w + QK_ROPE_DIM * hd + half_r: nope_w + QK_ROPE_DIM * (hd + 1)]
        q_ref[0, hd, QK_NOPE_DIM:QK_NOPE_DIM + half_r, :] = (t1 * cos_t - t2 * sin_t).astype(BF16)
        q_ref[0, hd, QK_NOPE_DIM + half_r:QK_NOPE_DIM + QK_ROPE_DIM, :] = (t2 * cos_t + t1 * sin_t).astype(BF16)
        q_ref[0, hd, QK_NOPE_DIM + QK_ROPE_DIM:, :] = jnp.zeros((QK_PAD_DIM - QK_NOPE_DIM - QK_ROPE_DIM, tm), BF16)
    conv_stage(1)

    kvl = _dot(hb, win_ref[:, _C_KVLAT:_C_KR])
    kvn = (_rms(kvl) * kvg_ref[...]).astype(BF16)
    kn = _dot(kvn, wuk_ref[...])
    kr = _rope_pairs(_dot(hb, win_ref[:, _C_KR:_C_END]), cos, sin, first_half)
    kr = jnp.where(low, kr, 0.0).astype(BF16)
    for hd in range(N_HEADS):
        k_ref[0, hd, :, 0:LANES] = kn[:, LANES * hd: LANES * (hd + 1)].astype(BF16)
        k_ref[0, hd, :, LANES:QK_PAD_DIM] = kr
    conv_stage(2)

    vt = lax.dot_general(wuvt_ref[...], kvn, (((1,), (1,)), ((), ())), preferred_element_type=F32)
    for hd in range(N_HEADS):
        vt_ref[0, hd] = vt[V_HEAD_DIM * hd: V_HEAD_DIM * (hd + 1), :].astype(BF16)
    conv_stage(3)

    ya_ref[...] = _dot(act_ref[...], wco_ref[...]).astype(BF16)
    buf_ref[0:CONV_HALO] = buf_ref[tm:tm + CONV_HALO]


def _in_conv(x2, mod3, g_pre, w_in_r, q_norm_g, w_uqt, kv_norm_g, w_uk, w_uvt, pos_f, inv_freq,
             conv_w8, conv_b8, ln_g, ln_b, w_conv_out, batch, seq, tm, rows):
    nt = seq // tm
    q_scale = (1.0 / math.sqrt(QK_NOPE_DIM + QK_ROPE_DIM)) * math.log2(math.e)
    tok = lambda i: (i, 0)
    heads = lambda i: (i // nt, 0, i % nt, 0)
    return pl.pallas_call(
        functools.partial(_in_conv_kernel, q_scale=q_scale, tm=tm, nt=nt, rows=rows),
        grid=(batch * nt,),
        in_specs=[
            pl.BlockSpec((tm, D_MODEL), tok),
            pl.BlockSpec((1, 6, D_MODEL), lambda i: (i // nt, 0, 0)),
            _const_spec((1, D_MODEL)),
            _const_spec(w_in_r.shape),
            _const_spec((1, Q_LORA_RANK)),
            _const_spec(w_uqt.shape),
            _const_spec((1, KV_LORA_RANK)),
            _const_spec(w_uk.shape),
            _const_spec(w_uvt.shape),
            pl.BlockSpec((1, 1, tm), lambda i: (i, 0, 0)),
            _const_spec((QK_ROPE_DIM // 2, 1)),
            _const_spec(conv_w8.shape),
            _const_spec((SUBLANES, LANES)),
            _const_spec((1, CONV_DIM)),
            _const_spec((1, CONV_DIM)),
            _const_spec(w_conv_out.shape),
        ],
        out_specs=[
            pl.BlockSpec((tm, D_MODEL), tok),
            pl.BlockSpec((tm, 2 * D_MODEL), tok),
            pl.BlockSpec((1, N_HEADS, QK_PAD_DIM, tm), lambda i: (i // nt, 0, 0, i % nt)),
            pl.BlockSpec((1, N_HEADS, tm, QK_PAD_DIM), heads),
            pl.BlockSpec((1, N_HEADS, V_HEAD_DIM, tm), lambda i: (i // nt, 0, 0, i % nt)),
        ],
        out_shape=[
            jax.ShapeDtypeStruct((batch * seq, D_MODEL), BF16),
            jax.ShapeDtypeStruct((batch * seq, 2 * D_MODEL), BF16),
            jax.ShapeDtypeStruct((batch, N_HEADS, QK_PAD_DIM, seq), BF16),
            jax.ShapeDtypeStruct((batch, N_HEADS, seq, QK_PAD_DIM), BF16),
            jax.ShapeDtypeStruct((batch, N_HEADS, V_HEAD_DIM, seq), BF16),
        ],
        scratch_shapes=[pltpu.VMEM((CONV_HALO + tm, SUBLANES, LANES), F32),
                        pltpu.VMEM((tm, CONV_DIM), BF16)],
        compiler_params=_params("arbitrary"),
        name="in_conv",
    )(x2, mod3, g_pre, w_in_r, q_norm_g, w_uqt, kv_norm_g, w_uk, w_uvt,
      pos_f.reshape(batch * nt, 1, tm), inv_freq,
      conv_w8, conv_b8, ln_g, ln_b, w_conv_out)


def _attn_kernel(qt_ref, k_ref, vt_ref, *rest, tq, tk, hpb, n_weights):
    w_refs, o_ref, wb_refs = rest[:n_weights], rest[n_weights], rest[n_weights + 1:2 * n_weights + 1]
    m_ref, acc_ref, s_ref = rest[2 * n_weights + 1:]
    for w_ref, wb_ref in zip(w_refs, wb_refs):
        wb_ref[...] = w_ref[...].astype(BF16)
    qi = pl.program_id(2)
    n_diag = tq // tk
    n_full = qi * n_diag
    m_ref[...] = jnp.full(m_ref.shape, -1e30, F32)
    acc_ref[...] = jnp.zeros(acc_ref.shape, F32)

    def tile_start(j):
        return pl.multiple_of(j * tk, tk)

    def scores(hd, j, q0=0):
        k = k_ref[0, hd, pl.ds(tile_start(j), tk), :]
        s_ref[hd, :, q0:] = _dot(k, qt_ref[0, hd, :, q0:])

    def update(hd, j, q0=0, diag=False):
        s = s_ref[hd, :, q0:]
        if diag:
            kpos = lax.broadcasted_iota(jnp.int32, (tk, tk), 0)
            qpos = lax.broadcasted_iota(jnp.int32, (tk, tk), 1)
            tri = jnp.where(qpos >= kpos, s[:, :tk], -1e30)
            s = tri if q0 + tk == tq else jnp.concatenate([tri, s[:, tk:]], axis=1)
        m_old = m_ref[hd, :, q0:]
        m_new = jnp.maximum(m_old, jnp.max(s, axis=0, keepdims=True))
        alpha = jnp.exp2(m_old - m_new)
        p = jnp.exp2(s - m_new)
        m_ref[hd, :, q0:] = m_new
        vt = jnp.concatenate([vt_ref[0, hd, :, pl.ds(tile_start(j), tk)], jnp.ones((DENOM_ROWS, tk), BF16)],
                             axis=0)
        acc_ref[hd, :, q0:] = alpha * acc_ref[hd, :, q0:] + _dot(vt, p.astype(BF16))

    scores(0, 0)

    def full_round(j, carry):
        for hd in range(hpb):
            if hd + 1 < hpb:
                scores(hd + 1, j)
            else:
                scores(0, j + 1)
            update(hd, j)
        return carry

    lax.fori_loop(0, n_full, full_round, 0)
    for d in range(n_diag):
        for hd in range(hpb):
            if hd + 1 < hpb:
                scores(hd + 1, n_full + d, d * tk)
            elif d + 1 < n_diag:
                scores(0, n_full + d + 1, (d + 1) * tk)
            update(hd, n_full + d, d * tk, True)
    for hd in range(hpb):
        o = acc_ref[hd, :V_HEAD_DIM, :] / acc_ref[hd, V_HEAD_DIM:V_HEAD_DIM + 1, :]
        o_ref[0, :, V_HEAD_DIM * hd: V_HEAD_DIM * (hd + 1)] = o.T.astype(BF16)


def _attn(qt4, k4, vt4, later_weights, tq, tk, hpb):
    batch, heads, seq, _ = k4.shape
    grid = (batch, heads // hpb, seq // tq)
    steps = grid[0] * grid[1] * grid[2]
    slab = lambda b, h, i: ((b * grid[1] + h) * grid[2] + i, 0)
    w_specs = [pl.BlockSpec((w.shape[0] // steps, w.shape[1]), slab) for w in later_weights]
    outs = pl.pallas_call(
        functools.partial(_attn_kernel, tq=tq, tk=tk, hpb=hpb,
                          n_weights=len(later_weights)),
        grid=grid,
        in_specs=[
            pl.BlockSpec((1, hpb, QK_PAD_DIM, tq), lambda b, h, i: (b, h, 0, i)),
            pl.BlockSpec((1, hpb, seq, QK_PAD_DIM), lambda b, h, i: (b, h, 0, 0)),
            pl.BlockSpec((1, hpb, V_HEAD_DIM, seq), lambda b, h, i: (b, h, 0, 0)),
        ] + w_specs,
        out_specs=[pl.BlockSpec((1, tq, hpb * V_HEAD_DIM), lambda b, h, i: (b, i, h))] + w_specs,
        out_shape=[jax.ShapeDtypeStruct((batch, seq, heads * V_HEAD_DIM), BF16)]
        + [jax.ShapeDtypeStruct(w.shape, BF16) for w in later_weights],
        scratch_shapes=[pltpu.VMEM((hpb, 1, tq), F32),
                        pltpu.VMEM((hpb, V_HEAD_DIM + DENOM_ROWS, tq), F32),
                        pltpu.VMEM((hpb, tk, tq), F32)],
        compiler_params=_params("arbitrary", "arbitrary", "arbitrary"),
        name="attn",
    )(qt4, k4, vt4, *later_weights)
    return outs[0], outs[1:]


def _mix_mlp_kernel(o_ref, ya_ref, gate_ref, x_ref, mod_ref, gpost_mix_ref, gpre_mlp_ref, gpost_mlp_ref,
                    wao_ref, wout_ref, w1_ref, w2_ref, out_ref):
    x = x_ref[...]
    y_b = _dot(o_ref[...], wao_ref[...])
    g_a = jax.nn.sigmoid(gate_ref[:, :D_MODEL].astype(F32))
    g_b = jax.nn.sigmoid(gate_ref[:, D_MODEL:].astype(F32))
    merged = g_a * ya_ref[...].astype(F32) + g_b * y_b
    y = _dot(merged.astype(BF16), wout_ref[...])
    x1 = x + mod_ref[0, 2:3, :] * (_rms(y) * gpost_mix_ref[...])

    shift = mod_ref[0, 3:4, :]
    scale = mod_ref[0, 4:5, :]
    h = (_rms(x1) * gpre_mlp_ref[...]) * (1.0 + scale) + shift
    hid = jnp.maximum(_dot(h.astype(BF16), w1_ref[...]), 0.0)
    y2 = _dot((hid * hid).astype(BF16), w2_ref[...])
    out_ref[...] = x1 + mod_ref[0, 5:6, :] * (_rms(y2) * gpost_mlp_ref[...])


def _mix_mlp(o2, ya2, gates, x2, mod3, g_post_mix, g_pre_mlp, g_post_mlp, w_attn_out, w_out, w1, w2, seq, tm):
    tokens = x2.shape[0]
    nt = seq // tm
    tok = lambda i: (i, 0)
    return pl.pallas_call(
        _mix_mlp_kernel,
        grid=(tokens // tm,),
        in_specs=[
            pl.BlockSpec((tm, D_MODEL), tok),
            pl.BlockSpec((tm, D_MODEL), tok),
            pl.BlockSpec((tm, 2 * D_MODEL), tok),
            pl.BlockSpec((tm, D_MODEL), tok),
            pl.BlockSpec((1, 6, D_MODEL), lambda i: (i // nt, 0, 0)),
            _const_spec((1, D_MODEL)),
            _const_spec((1, D_MODEL)),
            _const_spec((1, D_MODEL)),
            _const_spec(w_attn_out.shape),
            _const_spec(w_out.shape),
            _const_spec(w1.shape),
            _const_spec(w2.shape),
        ],
        out_specs=pl.BlockSpec((tm, D_MODEL), tok),
        out_shape=jax.ShapeDtypeStruct((tokens, D_MODEL), F32),
        compiler_params=_params("arbitrary"),
        name="mix_mlp",
    )(o2, ya2, gates, x2, mod3, g_post_mix, g_pre_mlp, g_post_mlp, w_attn_out, w_out, w1, w2)


def _head_major_cols(w, split):
    k = w.shape[0]
    w3 = w.reshape(k, N_HEADS, -1)
    return jnp.concatenate([w3[:, :, :split].reshape(k, -1), w3[:, :, split:].reshape(k, -1)], axis=1)


def kernel(x, c, positions, w_ada, b_ada, g_pre_mix, g_post_mix, g_pre_mlp, g_post_mlp, w_in, conv_w, conv_b, conv_norm_g, conv_norm_b, w_conv_out, q_norm_g, w_uq, kv_norm_g, w_ukv, w_attn_out, w_out, w_mlp_in, w_mlp_out):
    batch, seq, d = x.shape
    depth = w_ada.shape[0]
    tokens = batch * seq
    row = lambda v: v.reshape(1, -1)

    inv_freq = 1.0 / (ROPE_THETA ** (jnp.arange(0, QK_ROPE_DIM, 2, dtype=F32) / QK_ROPE_DIM))
    inv_freq = inv_freq.reshape(-1, 1)
    pos_f = positions.astype(F32)
    c_pad = jnp.pad(c, ((0, 8 - batch), (0, 0)))

    x2 = x.reshape(tokens, d)
    for l in range(depth):
        w_in_r = _prep_w_in(w_in[l])
        w_uqt = _head_major_cols(w_uq[l], QK_NOPE_DIM).astype(BF16).T
        w_ukv_r = _head_major_cols(w_ukv[l], QK_NOPE_DIM).astype(BF16)
        w_uk = w_ukv_r[:, :N_HEADS * QK_NOPE_DIM]
        w_uvt = w_ukv_r[:, N_HEADS * QK_NOPE_DIM:].T

        mod = _ada(c_pad, w_ada[l], row(b_ada[l]))
        mod3 = mod[:batch].reshape(batch, 6, d)

        conv_w8 = conv_w[l].reshape(CONV_KERNEL, SUBLANES, LANES)
        conv_b8 = conv_b[l].reshape(SUBLANES, LANES)
        y_a, gates, q4, k4, vt4 = _in_conv(x2, mod3, row(g_pre_mix[l]), w_in_r, row(q_norm_g[l]), w_uqt,
                                           row(kv_norm_g[l]), w_uk, w_uvt, pos_f, inv_freq, conv_w8, conv_b8,
                                           row(conv_norm_g[l]), row(conv_norm_b[l]), w_conv_out[l].astype(BF16),
                                           batch, seq, tm=IN_CONV_ROWS, rows=CONV_CHUNK_ROWS)
        o, (w_ao, w_o, w_1, w_2) = _attn(q4, k4, vt4, (w_attn_out[l], w_out[l], w_mlp_in[l], w_mlp_out[l]),
                                         tq=ATTN_Q_ROWS, tk=ATTN_K_ROWS, hpb=ATTN_HEADS_PER_STEP)
        x2 = _mix_mlp(o.reshape(tokens, d), y_a, gates, x2, mod3, row(g_post_mix[l]), row(g_pre_mlp[l]),
                      row(g_post_mlp[l]), w_ao, w_o, w_1, w_2, seq, tm=MIX_MLP_ROWS)
    return x2.reshape(batch, seq, d)
```
